```python
import math
import jax
import jax.numpy as jnp
from jax import lax
import numpy as np

D_MODEL = 1024
BATCH = 2
SEQ = 8192
DEPTH = 2

GRID_W = 64
CTX_LEN = 256

F32 = jnp.float32
NORM_EPS = 1e-6

DN_ALPHA = (2.0 * DEPTH) ** 0.25
DN_BETA = (8.0 * DEPTH) ** -0.25

MLA_HEADS = 8
MLA_NOPE = 64
MLA_ROPE = 32
MLA_V = 64
MLA_Q_LORA = 384
MLA_KV_LORA = 256
MLA_SCALE = (MLA_NOPE + MLA_ROPE) ** -0.5
ROPE_BASE = 10000.0
Q_BLOCK = 128

S5_WIDTH = 512
S5_GROUP = 16
S5_GROUPS = S5_WIDTH // S5_GROUP
S5_STATE = 64
S5_DT_MIN = 1e-3
S5_DT_MAX = 1e-1

EVEN_IN = MLA_Q_LORA + MLA_KV_LORA + MLA_ROPE + S5_WIDTH
EVEN_MIX = MLA_HEADS * MLA_V + S5_WIDTH

HG_HEADS = 8
HG_DK = 128
HG_DV = D_MODEL // HG_HEADS
HG_WIDTH = HG_HEADS * HG_DK
HG_VWIDTH = HG_HEADS * HG_DV
HG_IN = 3 * HG_WIDTH + 2 * HG_VWIDTH
HG_CHUNK = 64

FFN_HIDDEN = 2816
CONV_WIDTH = 3

N_EVEN = (DEPTH + 1) // 2
N_ODD = DEPTH // 2

kernel_name = "hybrid_mla_s5_hgrn2_convffn_prefix_dit"


def _layer_norm(x, g, b):
    xf = x.astype(F32)
    mu = jnp.mean(xf, -1, keepdims=True)
    var = jnp.mean(jnp.square(xf - mu), -1, keepdims=True)
    y = (xf - mu) * lax.rsqrt(var + NORM_EPS)
    return (y * g.astype(F32) + b.astype(F32)).astype(x.dtype)


def _rms_norm(x, g):
    xf = x.astype(F32)
    y = xf * lax.rsqrt(jnp.mean(jnp.square(xf), -1, keepdims=True) + NORM_EPS)
    return (y * g.astype(F32)).astype(x.dtype)


def _axial_rope_tables(length):
    rows = length // GRID_W
    row = jnp.repeat(jnp.arange(rows, dtype=F32), GRID_W)
    col = jnp.tile(jnp.arange(GRID_W, dtype=F32), rows)
    n_freq = MLA_ROPE // 4
    inv = ROPE_BASE ** (-jnp.arange(n_freq, dtype=F32) / n_freq)
    ar = row[:, None] * inv
    ac = col[:, None] * inv
    ang = jnp.concatenate([ar, ar, ac, ac], axis=-1)
    return jnp.cos(ang), jnp.sin(ang)


def _rope(x, cos, sin):
    xs = x.reshape(x.shape[:-1] + (2, 2, MLA_ROPE // 4))
    rot = jnp.stack([-xs[..., 1, :], xs[..., 0, :]], axis=-2).reshape(x.shape)
    return (x * cos + rot * sin).astype(x.dtype)


def _mla_qkv(cq, ckv, kr, q_norm, w_uq, kv_norm, w_ukv, rope):
    b, l = cq.shape[:2]
    q = (_rms_norm(cq, q_norm) @ w_uq).reshape(b, l, MLA_HEADS, MLA_NOPE + MLA_ROPE)
    kv = (_rms_norm(ckv, kv_norm) @ w_ukv).reshape(b, l, MLA_HEADS, MLA_NOPE + MLA_V)
    q_nope, q_rope = q[..., :MLA_NOPE], q[..., MLA_NOPE:]
    k_nope, v = kv[..., :MLA_NOPE], kv[..., MLA_NOPE:]
    if rope is not None:
        cos, sin = rope
        q_rope = _rope(q_rope, cos[:, None], sin[:, None])
        kr = _rope(kr, cos, sin)
    q = jnp.concatenate([q_nope, q_rope], -1)
    k = jnp.concatenate([k_nope, jnp.broadcast_to(kr[:, :, None, :], (b, l, MLA_HEADS, MLA_ROPE))], -1)
    return q, k, v


def _softmax_attend(q, k, v):
    s = jnp.einsum('bqhd,bkhd->bhqk', q, k, preferred_element_type=F32) * MLA_SCALE
    p = jax.nn.softmax(s, axis=-1).astype(v.dtype)
    return jnp.einsum('bhqk,bkhd->bqhd', p, v)


def _blocked_attend(q, k, v):
    b, l, h, d = q.shape
    qb = q.reshape(b, l // Q_BLOCK, Q_BLOCK, h, d).transpose(1, 0, 2, 3, 4)
    ob = lax.map(lambda qq: _softmax_attend(qq, k, v), qb)
    return ob.transpose(1, 0, 2, 3, 4).reshape(b, l, h, v.shape[-1])


def _s5_discretize(lam_re, lam_im, log_dt, b_re, b_im):
    lam_re, lam_im = lam_re.astype(F32), lam_im.astype(F32)
    dt = jnp.exp(log_dt.astype(F32))[:, None]
    mag = jnp.exp(lam_re * dt)
    lb_re = mag * jnp.cos(lam_im * dt)
    lb_im = mag * jnp.sin(lam_im * dt)
    den = lam_re * lam_re + lam_im * lam_im
    nr = lb_re - 1.0
    fr = (nr * lam_re + lb_im * lam_im) / den
    fi = (lb_im * lam_re - nr * lam_im) / den
    b_re, b_im = b_re.astype(F32), b_im.astype(F32)
    bb_re = fr[..., None] * b_re - fi[..., None] * b_im
    bb_im = fr[..., None] * b_im + fi[..., None] * b_re
    return lb_re, lb_im, bb_re, bb_im


def _s5_combine(e1, e2):
    a1r, a1i, b1r, b1i = e1
    a2r, a2i, b2r, b2i = e2
    return (a2r * a1r - a2i * a1i,
            a2r * a1i + a2i * a1r,
            a2r * b1r - a2i * b1i + b2r,
            a2r * b1i + a2i * b1r + b2i)


def _s5_states(u, lb_re, lb_im, bb_re, bb_im, h0):
    x_re = jnp.einsum('blgp,gnp->blgn', u, bb_re)
    x_im = jnp.einsum('blgp,gnp->blgn', u, bb_im)
    if h0 is not None:
        h_re, h_im = h0
        x_re = x_re.at[:, 0].add(lb_re * h_re - lb_im * h_im)
        x_im = x_im.at[:, 0].add(lb_re * h_im + lb_im * h_re)
    a_re = jnp.broadcast_to(lb_re, x_re.shape)
    a_im = jnp.broadcast_to(lb_im, x_im.shape)
    _, _, s_re, s_im = lax.associative_scan(_s5_combine, (a_re, a_im, x_re, x_im), axis=1)
    return s_re, s_im


def _s5_bidirectional(u, disc, c_re, c_im, h0s, need_out):
    y = None
    finals = []
    for d in range(2):
        ud = u if d == 0 else jnp.flip(u, 1)
        s_re, s_im = _s5_states(ud, *disc[d], None if h0s is None else h0s[d])
        finals.append((s_re[:, -1], s_im[:, -1]))
        if need_out:
            yd = (jnp.einsum('blgn,gpn->blgp', s_re, c_re[d].astype(F32))
                  - jnp.einsum('blgn,gpn->blgp', s_im, c_im[d].astype(F32)))
            yd = yd if d == 0 else jnp.flip(yd, 1)
            y = yd if y is None else y + yd
    return y, finals


def _even_mixer(u_ctx, u_lat, w_in, q_norm, w_uq, kv_norm, w_ukv, lam_re, lam_im, log_dt,
                b_re, b_im, c_re, c_im, d_skip, w_glu, w_out, need_ctx):
    cuts = [MLA_Q_LORA, MLA_Q_LORA + MLA_KV_LORA, MLA_Q_LORA + MLA_KV_LORA + MLA_ROPE]
    cq_c, ckv_c, kr_c, s_c = jnp.split(u_ctx @ w_in, cuts, axis=-1)
    cq_l, ckv_l, kr_l, s_l = jnp.split(u_lat @ w_in, cuts, axis=-1)
    b, l = u_lat.shape[:2]
    lc = u_ctx.shape[1]
    rope = _axial_rope_tables(l)
    q_c, k_c, v_c = _mla_qkv(cq_c, ckv_c, kr_c, q_norm, w_uq, kv_norm, w_ukv, None)
    q_l, k_l, v_l = _mla_qkv(cq_l, ckv_l, kr_l, q_norm, w_uq, kv_norm, w_ukv, rope)
    k_all = jnp.concatenate([k_c, k_l], axis=1)
    v_all = jnp.concatenate([v_c, v_l], axis=1)
    att_l = _blocked_attend(q_l, k_all, v_all).reshape(b, l, MLA_HEADS * MLA_V)
    disc = [_s5_discretize(lam_re[d], lam_im[d], log_dt[d], b_re[d], b_im[d]) for d in range(2)]
    us_c = s_c.astype(F32).reshape(b, lc, S5_GROUPS, S5_GROUP)
    us_l = s_l.astype(F32).reshape(b, l, S5_GROUPS, S5_GROUP)
    ys_c, fin_c = _s5_bidirectional(us_c, disc, c_re, c_im, None, need_ctx)
    ys_l, _ = _s5_bidirectional(us_l, disc, c_re, c_im, fin_c, True)
    d_g = d_skip.astype(F32).reshape(S5_GROUPS, S5_GROUP)

    def s5_out(y, us):
        z = (y + d_g * us).reshape(us.shape[0], us.shape[1], S5_WIDTH)
        z = jax.nn.gelu(z).astype(u_lat.dtype)
        return z * jax.nn.sigmoid(z @ w_glu)

    y_lat = jnp.concatenate([att_l, s5_out(ys_l, us_l)], -1) @ w_out
    y_ctx = None
    if need_ctx:
        att_c = _softmax_attend(q_c, k_c, v_c).reshape(b, lc, MLA_HEADS * MLA_V)
        y_ctx = jnp.concatenate([att_c, s5_out(ys_c, us_c)], -1) @ w_out
    return y_ctx, y_lat


def _hgrn_chunk_scan(q, k, v, logf, s0):
    b, l, h, _ = q.shape
    nc = l // HG_CHUNK

    def to_chunks(t):
        return t.reshape(b, nc, HG_CHUNK, h, t.shape[-1]).transpose(1, 0, 3, 2, 4)

    tri = jnp.tril(jnp.ones((HG_CHUNK, HG_CHUNK), dtype=bool))[:, :, None]

    def step(state, xs):
        qc, kc, vc, gc = xs
        cum = jnp.cumsum(gc, axis=-2)
        last = cum[..., -1:, :]
        dec = jnp.exp(jnp.where(tri, cum[..., :, None, :] - cum[..., None, :, :], -jnp.inf))
        scores = jnp.einsum('bhtk,bhsk,bhtsk->bhts', qc, kc, dec)
        o = (jnp.einsum('bhts,bhsv->bhtv', scores, vc)
             + jnp.einsum('bhtk,bhkv->bhtv', qc * jnp.exp(cum), state))
        state = (jnp.exp(last)[..., 0, :, None] * state
                 + jnp.einsum('bhsk,bhsv->bhkv', kc * jnp.exp(last - cum), vc))
        return state, o

    s_fin, o = lax.scan(step, s0, (to_chunks(q), to_chunks(k), to_chunks(v), to_chunks(logf)))
    o = o.transpose(1, 0, 3, 2, 4).reshape(b, l, h, v.shape[-1])
    return o, s_fin


def _hgrn_final_state(k, v, logf):
    tail = jnp.flip(jnp.cumsum(jnp.flip(logf, 1), axis=1), 1) - logf
    return jnp.einsum('blhk,blhv->bhkv', k * jnp.exp(tail), v)


def _odd_mixer(u_ctx, u_lat, w_in, lb, norm_g, w_out, need_ctx):
    cuts = [HG_WIDTH, 2 * HG_WIDTH, 3 * HG_WIDTH, 3 * HG_WIDTH + HG_VWIDTH]

    def prep(u):
        bb, ll = u.shape[:2]
        q, ff, fb, i, g = jnp.split(u @ w_in, cuts, axis=-1)
        heads = lambda t: t.astype(F32).reshape(bb, ll, HG_HEADS, -1)
        dirs = []
        for d, fpre in enumerate((ff, fb)):
            lbd = lb[d].reshape(HG_HEADS, HG_DK)
            f = lbd + (1.0 - lbd) * jax.nn.sigmoid(heads(fpre))
            dirs.append((1.0 - f, jnp.log(f)))
        return heads(q), heads(i), g, dirs

    q_c, v_c, g_c, dirs_c = prep(u_ctx)
    q_l, v_l, g_l, dirs_l = prep(u_lat)
    ident = lambda t: t
    flip = lambda t: jnp.flip(t, 1)
    o_c = None
    o_l = None
    for d in range(2):
        fl = ident if d == 0 else flip
        k_c, lf_c = dirs_c[d]
        k_l, lf_l = dirs_l[d]
        if need_ctx:
            s0 = jnp.zeros((u_ctx.shape[0], HG_HEADS, HG_DK, HG_DV), F32)
            oc, s_c = _hgrn_chunk_scan(fl(q_c), fl(k_c), fl(v_c), fl(lf_c), s0)
            o_c = fl(oc) if o_c is None else o_c + fl(oc)
        else:
            s_c = _hgrn_final_state(fl(k_c), fl(v_c), fl(lf_c))
        ol, _ = _hgrn_chunk_scan(fl(q_l), fl(k_l), fl(v_l), fl(lf_l), s_c)
        o_l = fl(ol) if o_l is None else o_l + fl(ol)

    def out(o, g):
        bb, ll = o.shape[:2]
        o = _rms_norm(o, norm_g).reshape(bb, ll, HG_VWIDTH)
        return (o * jax.nn.silu(g.astype(F32))).astype(g.dtype) @ w_out

    y_ctx = out(o_c, g_c) if need_ctx else None
    return y_ctx, out(o_l, g_l)


def _conv_ffn(u, w_in, conv_w, conv_b, w_out):
    a, gt = jnp.split(u @ w_in, 2, axis=-1)
    l = a.shape[1]
    pad = CONV_WIDTH // 2
    ap = jnp.pad(a, ((0, 0), (pad, pad), (0, 0)))
    conv = conv_b
    for j in range(CONV_WIDTH):
        conv = conv + conv_w[j] * ap[:, j:j + l]
    return (jax.nn.silu(conv) * gt) @ w_out


def setup_inputs(seed: int = 0) -> dict:
    key = jax.random.key(seed)
    keys = iter(jax.random.split(key, 48))

    def nrm(shape, scale):
        return jax.random.normal(next(keys), shape, F32) * scale

    def gain(shape):
        return 1.0 + nrm(shape, 0.02)

    D = D_MODEL
    x = nrm((BATCH, SEQ, D), 1.0)
    c = nrm((BATCH, D), 1.0)
    ctx = nrm((BATCH, CTX_LEN, D), 1.0)
    c_ctx = nrm((D,), 1.0)
    ada_w = nrm((DEPTH, D, 6 * D), D ** -0.5)
    ada_b = nrm((DEPTH, 6 * D), 0.01)
    ln_g = gain((DEPTH, 2, D))
    ln_b = nrm((DEPTH, 2, D), 0.01)
    ffn_w_in = nrm((DEPTH, D, 2 * FFN_HIDDEN), D ** -0.5)
    ffn_conv_w = nrm((DEPTH, CONV_WIDTH, FFN_HIDDEN), CONV_WIDTH ** -0.5)
    ffn_conv_b = nrm((DEPTH, FFN_HIDDEN), 0.01)
    ffn_w_out = nrm((DEPTH, FFN_HIDDEN, D), DN_BETA * FFN_HIDDEN ** -0.5)
    ev_w_in = nrm((N_EVEN, D, EVEN_IN), D ** -0.5)
    mla_q_norm = gain((N_EVEN, MLA_Q_LORA))
    mla_w_uq = nrm((N_EVEN, MLA_Q_LORA, MLA_HEADS * (MLA_NOPE + MLA_ROPE)), MLA_Q_LORA ** -0.5)
    mla_kv_norm = gain((N_EVEN, MLA_KV_LORA))
    mla_w_ukv = nrm((N_EVEN, MLA_KV_LORA, MLA_HEADS * (MLA_NOPE + MLA_V)), MLA_KV_LORA ** -0.5)
    s5_lam_re = -0.5 + nrm((N_EVEN, 2, S5_GROUPS, S5_STATE), 0.01)
    s5_lam_im = math.pi * jnp.arange(S5_STATE, dtype=F32) + nrm((N_EVEN, 2, S5_GROUPS, S5_STATE), 0.01)
    s5_log_dt = jax.random.uniform(next(keys), (N_EVEN, 2, S5_GROUPS), F32,
                                   math.log(S5_DT_MIN), math.log(S5_DT_MAX))
    s5_b_re = nrm((N_EVEN, 2, S5_GROUPS, S5_STATE, S5_GROUP), (2 * S5_GROUP) ** -0.5)
    s5_b_im = nrm((N_EVEN, 2, S5_GROUPS, S5_STATE, S5_GROUP), (2 * S5_GROUP) ** -0.5)
    s5_c_re = nrm((N_EVEN, 2, S5_GROUPS, S5_GROUP, S5_STATE), S5_STATE ** -0.5)
    s5_c_im = nrm((N_EVEN, 2, S5_GROUPS, S5_GROUP, S5_STATE), S5_STATE ** -0.5)
    s5_d = nrm((N_EVEN, S5_WIDTH), 1.0)
    s5_w_glu = nrm((N_EVEN, S5_WIDTH, S5_WIDTH), S5_WIDTH ** -0.5)
    ev_w_out = nrm((N_EVEN, EVEN_MIX, D), DN_BETA * EVEN_MIX ** -0.5)
    hg_w_in = nrm((N_ODD, D, HG_IN), D ** -0.5)
    hg_lb = nrm((DEPTH, 2, HG_WIDTH), 0.1)
    hg_norm = gain((N_ODD, HG_DV))
    hg_w_out = nrm((N_ODD, HG_VWIDTH, D), DN_BETA * HG_VWIDTH ** -0.5)
    return {"x": x, "c": c, "ctx": ctx, "c_ctx": c_ctx, "ada_w": ada_w, "ada_b": ada_b,
            "ln_g": ln_g, "ln_b": ln_b, "ffn_w_in": ffn_w_in, "ffn_conv_w": ffn_conv_w,
            "ffn_conv_b": ffn_conv_b, "ffn_w_out": ffn_w_out, "ev_w_in": ev_w_in,
            "mla_q_norm": mla_q_norm, "mla_w_uq": mla_w_uq, "mla_kv_norm": mla_kv_norm,
            "mla_w_ukv": mla_w_ukv, "s5_lam_re": s5_lam_re, "s5_lam_im": s5_lam_im,
            "s5_log_dt": s5_log_dt, "s5_b_re": s5_b_re, "s5_b_im": s5_b_im,
            "s5_c_re": s5_c_re, "s5_c_im": s5_c_im, "s5_d": s5_d, "s5_w_glu": s5_w_glu,
            "ev_w_out": ev_w_out, "hg_w_in": hg_w_in, "hg_lb": hg_lb, "hg_norm": hg_norm,
            "hg_w_out": hg_w_out}


def reference(x, c, ctx, c_ctx, ada_w, ada_b, ln_g, ln_b, ffn_w_in, ffn_conv_w, ffn_conv_b,
              ffn_w_out, ev_w_in, mla_q_norm, mla_w_uq, mla_kv_norm, mla_w_ukv, s5_lam_re,
              s5_lam_im, s5_log_dt, s5_b_re, s5_b_im, s5_c_re, s5_c_im, s5_d, s5_w_glu,
              ev_w_out, hg_w_in, hg_lb, hg_norm, hg_w_out):
    sm = jax.nn.softmax(hg_lb.astype(F32), axis=0)
    lower_bounds = jnp.cumsum(sm, axis=0) - sm[0]
    s_lat = jax.nn.silu(c)
    s_ctx = jax.nn.silu(c_ctx)
    for layer in range(DEPTH):
        need_ctx = layer < DEPTH - 1
        mod_l = s_lat @ ada_w[layer] + ada_b[layer]
        mod_c = s_ctx @ ada_w[layer] + ada_b[layer]
        sh_m, sc_m, g_m, sh_f, sc_f, g_f = jnp.split(mod_l[:, None, :], 6, axis=-1)
        csh_m, csc_m, cg_m, csh_f, csc_f, cg_f = jnp.split(mod_c, 6, axis=-1)
        u_lat = x * (1.0 + sc_m) + sh_m
        u_ctx = ctx * (1.0 + csc_m) + csh_m
        if layer % 2 == 0:
            e = layer // 2
            y_ctx, y_lat = _even_mixer(u_ctx, u_lat, ev_w_in[e], mla_q_norm[e], mla_w_uq[e],
                                       mla_kv_norm[e], mla_w_ukv[e], s5_lam_re[e], s5_lam_im[e],
                                       s5_log_dt[e], s5_b_re[e], s5_b_im[e], s5_c_re[e],
                                       s5_c_im[e], s5_d[e], s5_w_glu[e], ev_w_out[e], need_ctx)
        else:
            o = layer // 2
            y_ctx, y_lat = _odd_mixer(u_ctx, u_lat, hg_w_in[o], lower_bounds[layer], hg_norm[o],
                                      hg_w_out[o], need_ctx)
        x = _layer_norm(DN_ALPHA * x + g_m * y_lat, ln_g[layer, 0], ln_b[layer, 0])
        f_lat = _conv_ffn(x * (1.0 + sc_f) + sh_f, ffn_w_in[layer], ffn_conv_w[layer],
                          ffn_conv_b[layer], ffn_w_out[layer])
        x = _layer_norm(DN_ALPHA * x + g_f * f_lat, ln_g[layer, 1], ln_b[layer, 1])
        if need_ctx:
            ctx = _layer_norm(DN_ALPHA * ctx + cg_m * y_ctx, ln_g[layer, 0], ln_b[layer, 0])
            f_ctx = _conv_ffn(ctx * (1.0 + csc_f) + csh_f, ffn_w_in[layer], ffn_conv_w[layer],
                              ffn_conv_b[layer], ffn_w_out[layer])
            ctx = _layer_norm(DN_ALPHA * ctx + cg_f * f_ctx, ln_g[layer, 1], ln_b[layer, 1])
    return x
```

```python
import functools
import math

import jax
import jax.numpy as jnp
from jax import lax
from jax.experimental import pallas as pl
from jax.experimental.pallas import tpu as pltpu

F32 = jnp.float32
BF16 = jnp.bfloat16

D_MODEL = 1024
DEPTH = 2
GRID_W = 64
NORM_EPS = 1e-6
DN_ALPHA = (2.0 * DEPTH) ** 0.25

MLA_HEADS = 8
MLA_NOPE = 64
MLA_ROPE = 32
MLA_V = 64
MLA_Q_LORA = 384
MLA_KV_LORA = 256
MLA_SCALE = (MLA_NOPE + MLA_ROPE) ** -0.5
ROPE_BASE = 10000.0

S5_WIDTH = 512
S5_GROUP = 16
S5_GROUPS = S5_WIDTH // S5_GROUP
S5_STATE = 64
S5_CHUNK = 16

HG_HEADS = 8
HG_DK = 128
HG_DV = 128
HG_WIDTH = HG_HEADS * HG_DK
HG_IN = 5 * HG_WIDTH
HG_CHUNK = 128

FFN_HIDDEN = 2816

LANE = 128
HEAD_PAD = 128
EVEN_IN_PAD = MLA_Q_LORA + MLA_KV_LORA + S5_WIDTH + 2 * LANE

VMEM_LIMIT = 56 * 1024 * 1024


def _params(*sem):
    return pltpu.CompilerParams(dimension_semantics=sem, vmem_limit_bytes=VMEM_LIMIT)


def _const_spec(shape):
    zeros = (0,) * len(shape)
    return pl.BlockSpec(shape, lambda *_: zeros, pipeline_mode=pl.Buffered(1))


def _dot(a, b):
    return jnp.dot(a, b, preferred_element_type=F32)


def _dot_nt(a, b):
    return lax.dot_general(a, b, (((1,), (1,)), ((), ())), preferred_element_type=F32)


def _dot_tn(a, b):
    return lax.dot_general(a, b, (((0,), (0,)), ((), ())), preferred_element_type=F32)


def _sigmoid(x):
    return 1.0 / (1.0 + jnp.exp(-x))


def _silu(x):
    return x * _sigmoid(x)


def _rms(x, g):
    return x * lax.rsqrt(jnp.mean(x * x, -1, keepdims=True) + NORM_EPS) * g


def _layer_norm(x, g, b):
    mu = jnp.mean(x, -1, keepdims=True)
    xc = x - mu
    var = jnp.mean(xc * xc, -1, keepdims=True)
    return xc * lax.rsqrt(var + NORM_EPS) * g + b


ADA_TN = 512
N_COND = 3


def _adaln_kernel(ct_ref, w_ref, b_ref, o_ref):
    s = _silu(ct_ref[...])
    w = w_ref[0]
    rows = [jnp.sum(s[:, r:r + 1] * w, axis=0, keepdims=True) for r in range(N_COND)]
    rows.append(jnp.zeros((8 - N_COND, w.shape[1]), F32))
    o_ref[0] = jnp.concatenate(rows, axis=0) + b_ref[0]


def _adaln(cond_t, ada_w, ada_b):
    depth, d, n = ada_w.shape
    return pl.pallas_call(
        _adaln_kernel,
        grid=(depth, n // ADA_TN),
        in_specs=[
            pl.BlockSpec((d, 8), lambda l, j: (0, 0)),
            pl.BlockSpec((1, d, ADA_TN), lambda l, j: (l, 0, j)),
            pl.BlockSpec((1, 1, ADA_TN), lambda l, j: (l, 0, j)),
        ],
        out_specs=pl.BlockSpec((1, 8, ADA_TN), lambda l, j: (l, 0, j)),
        out_shape=jax.ShapeDtypeStruct((depth, 8, n), F32),
        compiler_params=_params("arbitrary", "arbitrary"),
        name="adaln",
    )(cond_t, ada_w, ada_b.reshape(depth, 1, n))


def _even_in_kernel(x_ref, mod_ref, win_ref, qg_ref, kvg_ref, wq_ref, wqr_ref, wk_ref, wv_ref,
                    cos_ref, sin_ref, q_ref, k_ref, v_ref, us_ref):
    m = mod_ref[0]
    u = (x_ref[0] * (1.0 + m[1:2]) + m[0:1]).astype(BF16)
    h = _dot(u, win_ref[...])
    c0 = MLA_Q_LORA
    c1 = c0 + MLA_KV_LORA
    c2 = c1 + S5_WIDTH
    us_ref[0] = h[:, c1:c2]
    cos = cos_ref[...]
    sin = sin_ref[...]
    kr = h[:, c2:c2 + LANE] * cos + h[:, c2 + LANE:c2 + 2 * LANE] * sin
    cqn = _rms(h[:, :c0], qg_ref[...]).astype(BF16)
    qa = _dot(cqn, wq_ref[...])
    qb = _dot(cqn, wqr_ref[...])
    kvn = _rms(h[:, c0:c1], kvg_ref[...]).astype(BF16)
    ka = _dot(kvn, wk_ref[...])
    v_ref[0] = _dot(kvn, wv_ref[...]).astype(BF16)
    for hh in range(MLA_HEADS):
        sl = slice(hh * HEAD_PAD, (hh + 1) * HEAD_PAD)
        q_ref[0, :, sl] = ((qa[:, sl] * cos + qb[:, sl] * sin) * MLA_SCALE).astype(BF16)
        k_ref[0, :, sl] = (ka[:, sl] + kr).astype(BF16)


def _even_in(x, mod, w, cos, sin, tm):
    b, l, d = x.shape
    hp = MLA_HEADS * HEAD_PAD
    row = lambda n: pl.BlockSpec((1, tm, n), lambda bi, i: (bi, i, 0))
    return pl.pallas_call(
        _even_in_kernel,
        grid=(b, l // tm),
        in_specs=[
            row(d),
            pl.BlockSpec((1, 8, d), lambda bi, i: (bi, 0, 0)),
            _const_spec(w["w_in"].shape), _const_spec(w["q_norm"].shape), _const_spec(w["kv_norm"].shape),
            _const_spec(w["wq"].shape), _const_spec(w["wq_rot"].shape), _const_spec(w["wk"].shape),
            _const_spec(w["wv"].shape),
            pl.BlockSpec((tm, HEAD_PAD), lambda bi, i: (i, 0)),
            pl.BlockSpec((tm, HEAD_PAD), lambda bi, i: (i, 0)),
        ],
        out_specs=[row(hp), row(hp), row(MLA_HEADS * MLA_V), row(S5_WIDTH)],
        out_shape=[
            jax.ShapeDtypeStruct((b, l, hp), BF16),
            jax.ShapeDtypeStruct((b, l, hp), BF16),
            jax.ShapeDtypeStruct((b, l, MLA_HEADS * MLA_V), BF16),
            jax.ShapeDtypeStruct((b, l, S5_WIDTH), F32),
        ],
        compiler_params=_params("parallel", "parallel"),
        name="even_in",
    )(x, mod, w["w_in"], w["q_norm"], w["kv_norm"], w["wq"], w["wq_rot"], w["wk"], w["wv"], cos, sin)


ATT_TQ = 256
ATT_TK = 512


def _attn_kernel(q_ref, *refs, kv_lens):
    o_ref = refs[-1]
    kv_refs = refs[:-1]
    tq = q_ref.shape[1]
    lane = lax.broadcasted_iota(jnp.int32, (1, 2 * MLA_V), 1)
    out = jnp.zeros((tq, 2 * MLA_V), F32)
    for hh in range(2):
        hsl = slice(hh * HEAD_PAD, (hh + 1) * HEAD_PAD)
        qh = q_ref[0, :, hsl]
        vmask = (lane // MLA_V) == hh

        def step(kc, vc, carry):
            m, l, acc = carry
            s = _dot_nt(qh, kc)
            m_new = jnp.maximum(m, jnp.max(s, -1, keepdims=True))
            alpha = jnp.exp(m - m_new)
            p = jnp.exp(s - m_new)
            l = alpha * l + jnp.sum(p, -1, keepdims=True)
            vm = jnp.where(vmask, vc, jnp.zeros_like(vc))
            acc = alpha * acc + _dot(p.astype(BF16), vm)
            return m_new, l, acc

        carry = (jnp.full((tq, 1), -jnp.inf, F32), jnp.zeros((tq, 1), F32),
                 jnp.zeros((tq, 2 * MLA_V), F32))
        for src, n in enumerate(kv_lens):
            k_ref, v_ref = kv_refs[2 * src], kv_refs[2 * src + 1]
            tk = min(ATT_TK, n)
            if n == tk:
                carry = step(k_ref[0, :, hsl], v_ref[0], carry)
            else:
                def body(j, c, k_ref=k_ref, v_ref=v_ref, tk=tk):
                    rows = pl.ds(pl.multiple_of(j * tk, tk), tk)
                    return step(k_ref[0, rows, hsl], v_ref[0, rows, :], c)
                carry = lax.fori_loop(0, n // tk, body, carry)
        m, l, acc = carry
        out = out + acc / l
    o_ref[0] = out.astype(BF16)


def _attention(q, kvs):
    b, lq, _ = q.shape
    tq = min(ATT_TQ, lq)
    in_specs = [pl.BlockSpec((1, tq, 2 * HEAD_PAD), lambda bi, hp, i: (bi, i, hp))]
    args = [q]
    for k, v in kvs:
        n = k.shape[1]
        in_specs.append(pl.BlockSpec((1, n, 2 * HEAD_PAD), lambda bi, hp, i: (bi, 0, hp)))
        in_specs.append(pl.BlockSpec((1, n, 2 * MLA_V), lambda bi, hp, i: (bi, 0, hp)))
        args += [k, v]
    return pl.pallas_call(
        functools.partial(_attn_kernel, kv_lens=tuple(k.shape[1] for k, _ in kvs)),
        grid=(b, MLA_HEADS // 2, lq // tq),
        in_specs=in_specs,
        out_specs=pl.BlockSpec((1, tq, 2 * MLA_V), lambda bi, hp, i: (bi, i, hp)),
        out_shape=jax.ShapeDtypeStruct((b, lq, MLA_HEADS * MLA_V), BF16),
        compiler_params=_params("parallel", "parallel", "parallel"),
        name="attention",
    )(*args)


def _s5_x_kernel(u_ref, w_ref, x_ref):
    x_ref[...] = _dot(u_ref[0], w_ref[0])


def _s5_chunk_inputs(u, wx):
    g, r, n = u.shape
    return pl.pallas_call(
        _s5_x_kernel,
        grid=(g,),
        in_specs=[pl.BlockSpec((1, r, n), lambda gi: (gi, 0, 0)),
                  pl.BlockSpec((1, n, n), lambda gi: (gi, 0, 0))],
        out_specs=pl.BlockSpec((r, n), lambda gi: (0, gi)),
        out_shape=jax.ShapeDtypeStruct((r, g * n), F32),
        compiler_params=_params("parallel"),
        name="s5_chunk_inputs",
    )(u, wx)


S5_SCAN_GROUPS = 8


def _s5_scan_kernel(x_ref, p1_ref, p2_ref, s_ref, *, n_ctx):
    n_tot = x_ref.shape[1]
    half = LANE // 2
    fw = slice(0, LANE)
    bw = slice(LANE, 2 * LANE)
    p1 = p1_ref[...]
    p2 = p2_ref[...]

    def advance(s, x, sl):
        return s * p1[:, sl] + pltpu.roll(s, half, 1) * p2[:, sl] + x

    def fwd(c, s):
        s_ref[0, c, :, fw] = s
        return advance(s, x_ref[0, c, :, fw], fw)

    def bwd(c, s):
        s_ref[0, c, :, bw] = s
        return advance(s, x_ref[0, c, :, bw], bw)

    zero = jnp.zeros((x_ref.shape[2], LANE), F32)
    lax.fori_loop(0, n_tot, fwd, zero)
    s = lax.fori_loop(0, n_ctx, lambda i, s: bwd(n_ctx - 1 - i, s), zero)
    lax.fori_loop(0, n_tot - n_ctx, lambda i, s: bwd(n_tot - 1 - i, s), s)


def _s5_scan(x, p1, p2, n_ctx):
    b, n, g, w = x.shape
    gb = S5_SCAN_GROUPS
    return pl.pallas_call(
        functools.partial(_s5_scan_kernel, n_ctx=n_ctx),
        grid=(b, g // gb),
        in_specs=[pl.BlockSpec((1, n, gb, w), lambda bi, gi: (bi, 0, gi, 0)),
                  pl.BlockSpec((gb, w), lambda bi, gi: (gi, 0)),
                  pl.BlockSpec((gb, w), lambda bi, gi: (gi, 0))],
        out_specs=pl.BlockSpec((1, n, gb, w), lambda bi, gi: (bi, 0, gi, 0)),
        out_shape=jax.ShapeDtypeStruct(x.shape, F32),
        compiler_params=_params("parallel", "parallel"),
        name="s5_scan",
    )(x, p1, p2)


def _s5_y_kernel(u_ref, s_ref, w_ref, y_ref):
    n = u_ref.shape[2]
    y_ref[0] = _dot(u_ref[0], w_ref[0, :n]) + _dot(s_ref[...].astype(BF16), w_ref[0, n:])


def _s5_outputs(u, s, wy):
    g, r, n = u.shape
    return pl.pallas_call(
        _s5_y_kernel,
        grid=(g,),
        in_specs=[pl.BlockSpec((1, r, n), lambda gi: (gi, 0, 0)),
                  pl.BlockSpec((r, n), lambda gi: (0, gi)),
                  pl.BlockSpec((1, 2 * n, n), lambda gi: (gi, 0, 0))],
        out_specs=pl.BlockSpec((1, r, n), lambda gi: (gi, 0, 0)),
        out_shape=jax.ShapeDtypeStruct((g, r, n), F32),
        compiler_params=_params("parallel"),
        name="s5_outputs",
    )(u, s, wy)


def _cmul(ar, ai, br, bi):
    return ar * br - ai * bi, ar * bi + ai * br


def _s5_weights(lam_re, lam_im, log_dt, b_re, b_im, c_re, c_im):
    t = S5_CHUNK
    hi = lax.Precision.HIGHEST
    dt = jnp.exp(log_dt)[..., None]
    mag = jnp.exp(lam_re * dt)
    lb_re = mag * jnp.cos(lam_im * dt)
    lb_im = mag * jnp.sin(lam_im * dt)
    den = lam_re * lam_re + lam_im * lam_im
    nr = lb_re - 1.0
    fr = (nr * lam_re + lb_im * lam_im) / den
    fi = (lb_im * lam_re - nr * lam_im) / den
    bb_re = fr[..., None] * b_re - fi[..., None] * b_im
    bb_im = fr[..., None] * b_im + fi[..., None] * b_re
    pw = [(jnp.ones_like(lb_re), jnp.zeros_like(lb_im))]
    for _ in range(t):
        pw.append(_cmul(pw[-1][0], pw[-1][1], lb_re, lb_im))
    pw_re = jnp.stack([p[0] for p in pw])
    pw_im = jnp.stack([p[1] for p in pw])
    pb_re, pb_im = _cmul(pw_re[:t, ..., None], pw_im[:t, ..., None], bb_re[None], bb_im[None])
    kern = (jnp.einsum('dgpn,ldgnq->ldgpq', c_re, pb_re, precision=hi)
            - jnp.einsum('dgpn,ldgnq->ldgpq', c_im, pb_im, precision=hi))
    s_idx = jnp.arange(t)[:, None]
    t_idx = jnp.arange(t)[None, :]
    lag = t_idx - s_idx
    kf = jnp.where((lag >= 0)[..., None, None, None], kern[jnp.clip(lag, 0, t - 1), 0], 0.0)
    kb = jnp.where((lag <= 0)[..., None, None, None], kern[jnp.clip(-lag, 0, t - 1), 1], 0.0)
    g = lam_re.shape[1]
    n_lane = t * S5_GROUP
    toep = (kf + kb).transpose(2, 0, 4, 1, 3).reshape(g, n_lane, n_lane)

    def to_state(re, im):
        both = jnp.concatenate([re, im], axis=2)
        return both.transpose(1, 0, 3, 2).reshape(g, n_lane, 2 * S5_STATE)

    wx = jnp.concatenate([to_state(pb_re[::-1, 0], pb_im[::-1, 0]),
                          to_state(pb_re[:, 1], pb_im[:, 1])], axis=-1)

    def from_state(d, p_re, p_im):
        cr = c_re[d][None]
        ci = c_im[d][None]
        a = cr * p_re[:, :, None, :] - ci * p_im[:, :, None, :]
        bneg = -(cr * p_im[:, :, None, :] + ci * p_re[:, :, None, :])
        both = jnp.concatenate([a, bneg], axis=-1)
        return both.transpose(1, 3, 0, 2).reshape(g, 2 * S5_STATE, n_lane)

    cf = from_state(0, pw_re[1:, 0], pw_im[1:, 0])
    cb = from_state(1, pw_re[t:0:-1, 1], pw_im[t:0:-1, 1])
    wy = jnp.concatenate([toep, cf, cb], axis=1)
    lt_re, lt_im = pw_re[t], pw_im[t]
    p1 = jnp.concatenate([lt_re[0], lt_re[0], lt_re[1], lt_re[1]], axis=-1)
    p2 = jnp.concatenate([-lt_im[0], lt_im[0], -lt_im[1], lt_im[1]], axis=-1)
    return wx.astype(BF16), wy.astype(BF16), p1, p2


def _s5_layer(us_ctx, us_lat, w):
    b, lc, _ = us_ctx.shape
    l = us_lat.shape[1]
    t, g, p = S5_CHUNK, S5_GROUPS, S5_GROUP

    def to_groups(u):
        n = u.shape[1]
        return u.astype(BF16).reshape(b, n // t, t, g, p).transpose(3, 0, 1, 2, 4).reshape(g, b, n // t, t * p)

    n_ctx, n_lat = lc // t, l // t
    n_tot = n_ctx + n_lat
    u = jnp.concatenate([to_groups(us_ctx), to_groups(us_lat)], axis=2).reshape(g, b * n_tot, t * p)
    x = _s5_chunk_inputs(u, w["wx"])
    s = _s5_scan(x.reshape(b, n_tot, g, 2 * LANE), w["p1"], w["p2"], n_ctx)
    y = _s5_outputs(u, s.reshape(b * n_tot, g * 2 * LANE), w["wy"])
    y = y.reshape(g, b, n_tot, t, p)

    def from_groups(yg):
        return yg.transpose(1, 2, 3, 0, 4).reshape(b, yg.shape[2] * t, g * p)

    return from_groups(y[:, :, :n_ctx]), from_groups(y[:, :, n_ctx:])


FFN_CHUNK = FFN_HIDDEN // 2


def _mid_tail(x, y, m, wffn_ref, lng_ref, lnb_ref, x1_ref, a_ref, gt_ref):
    x1 = _layer_norm(DN_ALPHA * x + m[2:3] * y, lng_ref[...], lnb_ref[...])
    x1_ref[0] = x1
    uf = (x1 * (1.0 + m[4:5]) + m[3:4]).astype(BF16)
    for j in range(FFN_HIDDEN // FFN_CHUNK):
        sl = slice(j * FFN_CHUNK, (j + 1) * FFN_CHUNK)
        a_ref[0, :, sl] = _dot(uf, wffn_ref[:, sl]).astype(BF16)
        gsl = slice(FFN_HIDDEN + j * FFN_CHUNK, FFN_HIDDEN + (j + 1) * FFN_CHUNK)
        gt_ref[0, :, sl] = _dot(uf, wffn_ref[:, gsl]).astype(BF16)


def _gelu_tanh(x):
    return 0.5 * x * (1.0 + jnp.tanh(math.sqrt(2.0 / math.pi) * (x + 0.044715 * x * x * x)))


def _even_mid_kernel(x_ref, mod_ref, att_ref, ys_ref, us_ref, d_ref, wglu_ref, wout_ref, lng_ref, lnb_ref,
                     wffn_ref, x1_ref, a_ref, gt_ref):
    z = _gelu_tanh(ys_ref[0] + d_ref[...] * us_ref[0])
    s5 = z * _sigmoid(_dot(z.astype(BF16), wglu_ref[...]))
    n_att = att_ref.shape[2]
    y = _dot(att_ref[0], wout_ref[:n_att]) + _dot(s5.astype(BF16), wout_ref[n_att:])
    _mid_tail(x_ref[0], y, mod_ref[0], wffn_ref, lng_ref, lnb_ref, x1_ref, a_ref, gt_ref)


def _odd_mid_kernel(x_ref, mod_ref, of_ref, ob_ref, g_ref, hn_ref, wout_ref, lng_ref, lnb_ref,
                    wffn_ref, x1_ref, a_ref, gt_ref):
    o = of_ref[0] + ob_ref[0]
    g = g_ref[0]
    gate = g * _sigmoid(g)
    parts = []
    for hh in range(HG_HEADS):
        sl = slice(hh * HG_DV, (hh + 1) * HG_DV)
        parts.append((_rms(o[:, sl], hn_ref[...]) * gate[:, sl]).astype(BF16))
    y = _dot(jnp.concatenate(parts, axis=-1), wout_ref[...])
    _mid_tail(x_ref[0], y, mod_ref[0], wffn_ref, lng_ref, lnb_ref, x1_ref, a_ref, gt_ref)


MID_TM = 256


def _mid_call(kernel, name, x, mod, row_args, row_specs, const_args):
    b, l, d = x.shape
    tm = min(MID_TM, l)
    row = lambda n: pl.BlockSpec((1, tm, n), lambda bi, i: (bi, i, 0))
    in_specs = [row(d), pl.BlockSpec((1, 8, d), lambda bi, i: (bi, 0, 0))]
    in_specs += [spec(tm) for spec in row_specs]
    in_specs += [_const_spec(a.shape) for a in const_args]
    return pl.pallas_call(
        kernel,
        grid=(b, l // tm),
        in_specs=in_specs,
        out_specs=[row(d), row(FFN_HIDDEN), row(FFN_HIDDEN)],
        out_shape=[jax.ShapeDtypeStruct((b, l, d), F32),
                   jax.ShapeDtypeStruct((b, l, FFN_HIDDEN), BF16),
                   jax.ShapeDtypeStruct((b, l, FFN_HIDDEN), BF16)],
        compiler_params=_params("parallel", "parallel"),
        name=name,
    )(x, mod, *row_args, *const_args)


def _row_spec(n, lane_block=0):
    return lambda tm: pl.BlockSpec((1, tm, n), lambda bi, i: (bi, i, lane_block))


def _even_mid(x, mod, att, ys, us, w):
    return _mid_call(
        _even_mid_kernel, "even_mid", x, mod, [att, ys, us],
        [_row_spec(att.shape[2]), _row_spec(S5_WIDTH), _row_spec(S5_WIDTH)],
        [w["s5_d"], w["w_glu"], w["w_out"], w["ln_g0"], w["ln_b0"], w["ffn_w_in"]])


def _odd_mid(x, mod, o_f, o_b, hg, w):
    return _mid_call(
        _odd_mid_kernel, "odd_mid", x, mod, [o_f, o_b, hg],
        [_row_spec(HG_WIDTH), _row_spec(HG_WIDTH), _row_spec(HG_WIDTH, 4)],
        [w["hg_norm"], w["w_out"], w["ln_g0"], w["ln_b0"], w["ffn_w_in"]])


FFN_TM = 256
HALO = 8


def _ffn_out_kernel(a_ref, ap_ref, an_ref, gt_ref, x1_ref, mod_ref, cw_ref, cb_ref, wout_ref, lng_ref, lnb_ref,
                    o_ref):
    i = pl.program_id(1)
    tm = a_ref.shape[1]
    a = a_ref[0].astype(F32)
    row = lax.broadcasted_iota(jnp.int32, (tm, 1), 0)
    before = jnp.where(i == 0, 0.0, ap_ref[0, HALO - 1:HALO, :].astype(F32))
    after = jnp.where(i == pl.num_programs(1) - 1, 0.0, an_ref[0, 0:1, :].astype(F32))
    a_prev = jnp.where(row == 0, before, pltpu.roll(a, 1, 0))
    a_next = jnp.where(row == tm - 1, after, pltpu.roll(a, tm - 1, 0))
    cw = cw_ref[...]
    conv = cb_ref[...] + cw[0:1] * a_prev + cw[1:2] * a + cw[2:3] * a_next
    hidden = (_silu(conv) * gt_ref[0].astype(F32)).astype(BF16)
    f = _dot(hidden, wout_ref[...])
    m = mod_ref[0]
    o_ref[0] = _layer_norm(DN_ALPHA * x1_ref[0] + m[5:6] * f, lng_ref[...], lnb_ref[...])


def _ffn_out(a, gt, x1, mod, w):
    b, l, d = x1.shape
    tm = min(FFN_TM, l)
    per = tm // HALO
    last = l // HALO - 1
    row = lambda n: pl.BlockSpec((1, tm, n), lambda bi, i: (bi, i, 0))
    consts = [w["conv_w"], w["conv_b"], w["ffn_w_out"], w["ln_g1"], w["ln_b1"]]
    return pl.pallas_call(
        _ffn_out_kernel,
        grid=(b, l // tm),
        in_specs=[
            row(FFN_HIDDEN),
            pl.BlockSpec((1, HALO, FFN_HIDDEN), lambda bi, i: (bi, jnp.maximum(i * per - 1, 0), 0)),
            pl.BlockSpec((1, HALO, FFN_HIDDEN), lambda bi, i: (bi, jnp.minimum((i + 1) * per, last), 0)),
            row(FFN_HIDDEN), row(d),
            pl.BlockSpec((1, 8, d), lambda bi, i: (bi, 0, 0)),
        ] + [_const_spec(c.shape) for c in consts],
        out_specs=row(d),
        out_shape=jax.ShapeDtypeStruct((b, l, d), F32),
        compiler_params=_params("parallel", "parallel"),
        name="ffn_out",
    )(a, a, a, gt, x1, mod, *consts)


ODD_TM = 256


def _odd_in_kernel(x_ref, mod_ref, w_ref, o_ref):
    m = mod_ref[0]
    u = (x_ref[0] * (1.0 + m[1:2]) + m[0:1]).astype(BF16)
    for j in range(HG_IN // HG_WIDTH):
        sl = slice(j * HG_WIDTH, (j + 1) * HG_WIDTH)
        o_ref[0, :, sl] = _dot(u, w_ref[:, sl])


def _odd_in(x, mod, w_in):
    b, l, d = x.shape
    tm = min(ODD_TM, l)
    return pl.pallas_call(
        _odd_in_kernel,
        grid=(b, l // tm),
        in_specs=[pl.BlockSpec((1, tm, d), lambda bi, i: (bi, i, 0)),
                  pl.BlockSpec((1, 8, d), lambda bi, i: (bi, 0, 0)),
                  _const_spec(w_in.shape)],
        out_specs=pl.BlockSpec((1, tm, HG_IN), lambda bi, i: (bi, i, 0)),
        out_shape=jax.ShapeDtypeStruct((b, l, HG_IN), F32),
        compiler_params=_params("parallel", "parallel"),
        name="odd_in",
    )(x, mod, w_in)


def _hgrn_direction(q, pre, v, lb, state, level, reverse):
    c = q.shape[0]
    row = lax.broadcasted_iota(jnp.int32, q.shape, 0)
    pos = (c - 1 - row) if reverse else row

    def prev(x, k):
        return pltpu.roll(x, (c - k) if reverse else k, 0)

    def nxt(x, k):
        return pltpu.roll(x, k if reverse else (c - k), 0)

    f = lb + (1.0 - lb) * _sigmoid(pre)
    key = 1.0 - f
    cum = jnp.log(f)
    k = 1
    while k < c:
        cum = cum + jnp.where(pos >= k, prev(cum, k), 0.0)
        k *= 2
    last = cum[0:1] if reverse else cum[c - 1:c]

    qb = q.astype(BF16)
    kb = key.astype(BF16)
    scores = jnp.where(level == 0, _dot_nt(qb, kb), 0.0)
    z = cum
    half = 1
    lvl = 1
    while half < c:
        ref = jnp.where((pos & half) == 0, z, prev(z, half))
        e = jnp.exp(-jnp.abs(cum - ref))
        x = (jnp.where((pos & half) != 0, q, key) * e).astype(BF16)
        scores = jnp.where(level == lvl, _dot_nt(x, x), scores)
        z = jnp.where((pos & half) != 0, z, nxt(z, half))
        half *= 2
        lvl += 1

    o = _dot(scores.astype(BF16), v)
    o = o + _dot_nt((q * jnp.exp(cum)).astype(BF16), state.astype(BF16))
    kd = (key * jnp.exp(last - cum)).astype(BF16)
    new_state = jnp.exp(last) * state + _dot_tn(v, kd)
    return o, new_state


def _hgrn_kernel(qf_ref, pf_ref, vf_ref, qb_ref, pb_ref, vb_ref, lb_ref, lvf_ref, lvb_ref, s0_ref,
                 of_ref, ob_ref, sfin_ref, st_ref):
    ci = pl.program_id(2)

    @pl.when(ci == 0)
    def _():
        st_ref[...] = s0_ref[0, 0]

    o, s = _hgrn_direction(qf_ref[0], pf_ref[0], vf_ref[0].astype(BF16), lb_ref[0], st_ref[0],
                           lvf_ref[...], False)
    of_ref[0] = o
    st_ref[0] = s
    o, s = _hgrn_direction(qb_ref[0], pb_ref[0], vb_ref[0].astype(BF16), lb_ref[1], st_ref[1],
                           lvb_ref[...], True)
    ob_ref[0] = o
    st_ref[1] = s

    @pl.when(ci == pl.num_programs(2) - 1)
    def _():
        sfin_ref[0, 0] = st_ref[...]


def _hgrn_levels(c):
    p = jnp.arange(c, dtype=jnp.int32)
    x = p[:, None] ^ p[None, :]
    lvl = jnp.where(x == 0, 0, 32 - lax.clz(x))
    fwd = jnp.where(p[:, None] >= p[None, :], lvl, -1)
    bwd = jnp.where(p[:, None] <= p[None, :], lvl, -1)
    return fwd.astype(jnp.int32), bwd.astype(jnp.int32)


def _hgrn(hg, lb, s0):
    b, l, _ = hg.shape
    c = HG_CHUNK
    n = l // c
    nh = HG_HEADS
    lvf, lvb = _hgrn_levels(c)
    fw = lambda blk: pl.BlockSpec((1, c, HG_DK), lambda bi, h, ci: (bi, ci, blk * nh + h))
    bw = lambda blk: pl.BlockSpec((1, c, HG_DK), lambda bi, h, ci: (bi, n - 1 - ci, blk * nh + h))
    state_spec = pl.BlockSpec((1, 1, 2, HG_DV, HG_DK), lambda bi, h, ci: (bi, h, 0, 0, 0))
    return pl.pallas_call(
        _hgrn_kernel,
        grid=(b, nh, n),
        in_specs=[fw(0), fw(1), fw(3), bw(0), bw(2), bw(3),
                  pl.BlockSpec((2, 1, HG_DK), lambda bi, h, ci: (0, 0, h)),
                  pl.BlockSpec((c, c), lambda bi, h, ci: (0, 0)),
                  pl.BlockSpec((c, c), lambda bi, h, ci: (0, 0)),
                  state_spec],
        out_specs=[pl.BlockSpec((1, c, HG_DV), lambda bi, h, ci: (bi, ci, h)),
                   pl.BlockSpec((1, c, HG_DV), lambda bi, h, ci: (bi, n - 1 - ci, h)),
                   state_spec],
        out_shape=[jax.ShapeDtypeStruct((b, l, HG_WIDTH), F32),
                   jax.ShapeDtypeStruct((b, l, HG_WIDTH), F32),
                   jax.ShapeDtypeStruct((b, nh, 2, HG_DV, HG_DK), F32)],
        scratch_shapes=[pltpu.VMEM((2, HG_DV, HG_DK), F32)],
        compiler_params=_params("parallel", "parallel", "arbitrary"),
        name="hgrn",
    )(hg, hg, hg, hg, hg, hg, lb, lvf, lvb, s0)


def _rope_tables(length):
    rows = length // GRID_W
    row = jnp.repeat(jnp.arange(rows, dtype=F32), GRID_W)
    col = jnp.tile(jnp.arange(GRID_W, dtype=F32), rows)
    n_freq = MLA_ROPE // 4
    inv = ROPE_BASE ** (-jnp.arange(n_freq, dtype=F32) / n_freq)
    ar = row[:, None] * inv
    ac = col[:, None] * inv
    ang = jnp.concatenate([ar, ar, ac, ac], axis=-1)
    return jnp.cos(ang), jnp.sin(ang)


def _head_tables(cos, sin):
    n = cos.shape[0]
    pad = jnp.zeros((n, HEAD_PAD - MLA_NOPE - MLA_ROPE), F32)
    return (jnp.concatenate([jnp.ones((n, MLA_NOPE), F32), cos, pad], -1),
            jnp.concatenate([jnp.zeros((n, MLA_NOPE), F32), sin, pad], -1))


def _rotate_cols(w):
    ws = w.reshape(w.shape[:-1] + (2, 2, MLA_ROPE // 4))
    return jnp.stack([-ws[..., 1, :], ws[..., 0, :]], axis=-2).reshape(w.shape)


def _even_weights(ev_w_in, q_norm, w_uq, kv_norm, w_ukv):
    c0 = MLA_Q_LORA
    c1 = c0 + MLA_KV_LORA
    c2 = c1 + MLA_ROPE
    d = ev_w_in.shape[0]
    w_kr = ev_w_in[:, c1:c2]
    lo = jnp.zeros((d, MLA_NOPE), F32)
    hi = jnp.zeros((d, HEAD_PAD - MLA_NOPE - MLA_ROPE), F32)
    w_in = jnp.concatenate([ev_w_in[:, :c1], ev_w_in[:, c2:], lo, w_kr, hi, lo, _rotate_cols(w_kr), hi], axis=-1)
    wq = w_uq.reshape(c0, MLA_HEADS, MLA_NOPE + MLA_ROPE)
    zq = jnp.zeros((c0, MLA_HEADS, HEAD_PAD - MLA_NOPE - MLA_ROPE), F32)
    wq_pad = jnp.concatenate([wq, zq], -1).reshape(c0, MLA_HEADS * HEAD_PAD)
    wq_rot = jnp.concatenate([jnp.zeros((c0, MLA_HEADS, MLA_NOPE), F32), _rotate_cols(wq[..., MLA_NOPE:]), zq],
                             -1).reshape(c0, MLA_HEADS * HEAD_PAD)
    wkv = w_ukv.reshape(MLA_KV_LORA, MLA_HEADS, MLA_NOPE + MLA_V)
    wk = jnp.concatenate([wkv[..., :MLA_NOPE], jnp.zeros((MLA_KV_LORA, MLA_HEADS, HEAD_PAD - MLA_NOPE), F32)],
                         -1).reshape(MLA_KV_LORA, MLA_HEADS * HEAD_PAD)
    wv = wkv[..., MLA_NOPE:].reshape(MLA_KV_LORA, MLA_HEADS * MLA_V)
    return {"w_in": w_in.astype(BF16), "q_norm": q_norm.reshape(1, -1), "kv_norm": kv_norm.reshape(1, -1),
            "wq": wq_pad.astype(BF16), "wq_rot": wq_rot.astype(BF16), "wk": wk.astype(BF16),
            "wv": wv.astype(BF16)}


def _layer_mod(mod, layer, b):
    d = D_MODEL
    rows = mod[layer, :, :6 * d].reshape(8, 6, d)
    pad = jnp.zeros((2, d), F32)
    lat = jnp.stack([jnp.concatenate([rows[bi], pad], 0) for bi in range(b)])
    ctx = jnp.broadcast_to(jnp.concatenate([rows[b], pad], 0)[None], (b, 8, d))
    return lat, ctx


def kernel(x, c, ctx, c_ctx, ada_w, ada_b, ln_g, ln_b, ffn_w_in, ffn_conv_w, ffn_conv_b, ffn_w_out, ev_w_in,
           mla_q_norm, mla_w_uq, mla_kv_norm, mla_w_ukv, s5_lam_re, s5_lam_im, s5_log_dt, s5_b_re, s5_b_im,
           s5_c_re, s5_c_im, s5_d, s5_w_glu, ev_w_out, hg_w_in, hg_lb, hg_norm, hg_w_out):
    b, l, d = x.shape
    lc = ctx.shape[1]
    assert b + 1 == N_COND and d == D_MODEL and l % 512 == 0 and lc % 256 == 0

    cond_t = jnp.concatenate([c, c_ctx[None], jnp.zeros((8 - N_COND, d), F32)], 0).T
    mod = _adaln(cond_t, ada_w, ada_b)

    def ffn_weights(layer):
        return {"ln_g0": ln_g[layer, 0].reshape(1, d), "ln_b0": ln_b[layer, 0].reshape(1, d),
                "ln_g1": ln_g[layer, 1].reshape(1, d), "ln_b1": ln_b[layer, 1].reshape(1, d),
                "ffn_w_in": ffn_w_in[layer].astype(BF16),
                "conv_w": jnp.concatenate([ffn_conv_w[layer], jnp.zeros((5, FFN_HIDDEN), F32)], 0),
                "conv_b": ffn_conv_b[layer].reshape(1, FFN_HIDDEN),
                "ffn_w_out": ffn_w_out[layer].astype(BF16)}

    mod_lat, mod_ctx = _layer_mod(mod, 0, b)
    we = _even_weights(ev_w_in[0], mla_q_norm[0], mla_w_uq[0], mla_kv_norm[0], mla_w_ukv[0])
    cos_l, sin_l = _head_tables(*_rope_tables(l))
    cos_c, sin_c = _head_tables(jnp.ones((lc, MLA_ROPE), F32), jnp.zeros((lc, MLA_ROPE), F32))
    q_l, k_l, v_l, us_l = _even_in(x, mod_lat, we, cos_l, sin_l, min(512, l))
    q_c, k_c, v_c, us_c = _even_in(ctx, mod_ctx, we, cos_c, sin_c, min(512, lc))
    att_l = _attention(q_l, [(k_l, v_l), (k_c, v_c)])
    att_c = _attention(q_c, [(k_c, v_c)])
    wx, wy, p1, p2 = _s5_weights(s5_lam_re[0], s5_lam_im[0], s5_log_dt[0], s5_b_re[0], s5_b_im[0],
                                 s5_c_re[0], s5_c_im[0])
    ys_c, ys_l = _s5_layer(us_c, us_l, {"wx": wx, "wy": wy, "p1": p1, "p2": p2})
    w0 = ffn_weights(0)
    w0.update({"s5_d": s5_d[0].reshape(1, S5_WIDTH), "w_glu": s5_w_glu[0].astype(BF16),
               "w_out": ev_w_out[0].astype(BF16)})
    x1, a, gt = _even_mid(x, mod_lat, att_l, ys_l, us_l, w0)
    x = _ffn_out(a, gt, x1, mod_lat, w0)
    c1, a, gt = _even_mid(ctx, mod_ctx, att_c, ys_c, us_c, w0)
    ctx = _ffn_out(a, gt, c1, mod_ctx, w0)

    mod_lat, mod_ctx = _layer_mod(mod, 1, b)
    sm = jax.nn.softmax(hg_lb, axis=0)
    lower = (jnp.cumsum(sm, axis=0) - sm[0])[1].reshape(2, 1, HG_WIDTH)
    w_hg = hg_w_in[0].astype(BF16)
    hg_c = _odd_in(ctx, mod_ctx, w_hg)
    hg_l = _odd_in(x, mod_lat, w_hg)
    zero_state = jnp.zeros((b, HG_HEADS, 2, HG_DV, HG_DK), F32)
    _, _, s_ctx = _hgrn(hg_c, lower, zero_state)
    o_f, o_b, _ = _hgrn(hg_l, lower, s_ctx)
    w1 = ffn_weights(1)
    w1.update({"hg_norm": hg_norm[0].reshape(1, HG_DV), "w_out": hg_w_out[0].astype(BF16)})
    x1, a, gt = _odd_mid(x, mod_lat, o_f, o_b, hg_l, w1)
    return _ffn_out(a, gt, x1, mod_lat, w1)
```

```python
import functools
import math

import jax
import jax.numpy as jnp
from jax import lax
from jax.experimental import pallas as pl
from jax.experimental.pallas import tpu as pltpu

F32 = jnp.float32
BF16 = jnp.bfloat16

D_MODEL = 1024
DEPTH = 2
GRID_W = 64
NORM_EPS = 1e-6
DN_ALPHA = (2.0 * DEPTH) ** 0.25

MLA_HEADS = 8
MLA_NOPE = 64
MLA_ROPE = 32
MLA_V = 64
MLA_Q_LORA = 384
MLA_KV_LORA = 256
MLA_SCALE = (MLA_NOPE + MLA_ROPE) ** -0.5
Q_SCALE = MLA_SCALE * math.log2(math.e)
ROPE_BASE = 10000.0

S5_WIDTH = 512
S5_GROUP = 16
S5_GROUPS = S5_WIDTH // S5_GROUP
S5_STATE = 64
S5_CHUNK = 16

HG_HEADS = 8
HG_DK = 128
HG_DV = 128
HG_WIDTH = HG_HEADS * HG_DK
HG_IN = 5 * HG_WIDTH
HG_CHUNK = 128

FFN_HIDDEN = 2816

LANE = 128
HEAD_PAD = 128
EVEN_IN_PAD = MLA_Q_LORA + MLA_KV_LORA + S5_WIDTH + 2 * LANE

VMEM_LIMIT = 56 * 1024 * 1024


def _params(*sem):
    return pltpu.CompilerParams(dimension_semantics=sem, vmem_limit_bytes=VMEM_LIMIT)


def _const_spec(shape):
    zeros = (0,) * len(shape)
    return pl.BlockSpec(shape, lambda *_: zeros, pipeline_mode=pl.Buffered(1))


def _dot(a, b):
    return jnp.dot(a, b, preferred_element_type=F32)


def _dot_nt(a, b):
    return lax.dot_general(a, b, (((1,), (1,)), ((), ())), preferred_element_type=F32)


def _dot_tn(a, b):
    return lax.dot_general(a, b, (((0,), (0,)), ((), ())), preferred_element_type=F32)


def _sigmoid(x):
    return 1.0 / (1.0 + jnp.exp(-x))


def _silu(x):
    return x * _sigmoid(x)


def _rms(x, g):
    return x * lax.rsqrt(jnp.mean(x * x, -1, keepdims=True) + NORM_EPS) * g


def _layer_norm(x, g, b):
    mu = jnp.mean(x, -1, keepdims=True)
    xc = x - mu
    var = jnp.mean(xc * xc, -1, keepdims=True)
    return xc * lax.rsqrt(var + NORM_EPS) * g + b


ADA_TN = 512
N_COND = 3


def _adaln_kernel(ct_ref, w_ref, b_ref, o_ref):
    s = _silu(ct_ref[...])
    w = w_ref[0]
    rows = [jnp.sum(s[:, r:r + 1] * w, axis=0, keepdims=True) for r in range(N_COND)]
    rows.append(jnp.zeros((8 - N_COND, w.shape[1]), F32))
    o_ref[0] = jnp.concatenate(rows, axis=0) + b_ref[0]


def _adaln(cond_t, ada_w, ada_b):
    depth, d, n = ada_w.shape
    return pl.pallas_call(
        _adaln_kernel,
        grid=(depth, n // ADA_TN),
        in_specs=[
            pl.BlockSpec((d, 8), lambda l, j: (0, 0)),
            pl.BlockSpec((1, d, ADA_TN), lambda l, j: (l, 0, j)),
            pl.BlockSpec((1, 1, ADA_TN), lambda l, j: (l, 0, j)),
        ],
        out_specs=pl.BlockSpec((1, 8, ADA_TN), lambda l, j: (l, 0, j)),
        out_shape=jax.ShapeDtypeStruct((depth, 8, n), F32),
        compiler_params=_params("arbitrary", "arbitrary"),
        name="adaln",
    )(cond_t, ada_w, ada_b.reshape(depth, 1, n))


def _even_in_kernel(x_ref, mod_ref, win_ref, qg_ref, kvg_ref, wq_ref, wqr_ref, wk_ref, wvt_ref,
                    cos_ref, sin_ref, q_ref, k_ref, vt_ref, us_ref):
    m = mod_ref[0]
    u = (x_ref[0] * (1.0 + m[1:2]) + m[0:1]).astype(BF16)
    h = _dot(u, win_ref[...])
    c0 = MLA_Q_LORA
    c1 = c0 + MLA_KV_LORA
    c2 = c1 + S5_WIDTH
    us_ref[0] = h[:, c1:c2]
    cos = cos_ref[...]
    sin = sin_ref[...]
    kr = h[:, c2:c2 + LANE] * cos + h[:, c2 + LANE:c2 + 2 * LANE] * sin
    cqn = _rms(h[:, :c0], qg_ref[...]).astype(BF16)
    qa = _dot(cqn, wq_ref[...])
    qb = _dot(cqn, wqr_ref[...])
    kvn = _rms(h[:, c0:c1], kvg_ref[...]).astype(BF16)
    ka = _dot(kvn, wk_ref[...])
    vt = _dot_nt(wvt_ref[...], kvn).astype(BF16)
    ones = jnp.ones((V_ROWS - MLA_V, vt.shape[1]), BF16)
    for hh in range(MLA_HEADS):
        vt_ref[0, hh * V_ROWS:hh * V_ROWS + MLA_V] = vt[hh * MLA_V:(hh + 1) * MLA_V]
        vt_ref[0, hh * V_ROWS + MLA_V:(hh + 1) * V_ROWS] = ones
    for hh in range(MLA_HEADS):
        sl = slice(hh * HEAD_PAD, (hh + 1) * HEAD_PAD)
        q_ref[0, :, sl] = ((qa[:, sl] * cos + qb[:, sl] * sin) * Q_SCALE).astype(BF16)
        k_ref[0, :, sl] = (ka[:, sl] + kr).astype(BF16)


def _even_in(x, mod, w, cos, sin, tm):
    b, l, d = x.shape
    hp = MLA_HEADS * HEAD_PAD
    row = lambda n: pl.BlockSpec((1, tm, n), lambda bi, i: (bi, i, 0))
    return pl.pallas_call(
        _even_in_kernel,
        grid=(b, l // tm),
        in_specs=[
            row(d),
            pl.BlockSpec((1, 8, d), lambda bi, i: (bi, 0, 0)),
            _const_spec(w["w_in"].shape), _const_spec(w["q_norm"].shape), _const_spec(w["kv_norm"].shape),
            _const_spec(w["wq"].shape), _const_spec(w["wq_rot"].shape), _const_spec(w["wk"].shape),
            _const_spec(w["wv_t"].shape),
            pl.BlockSpec((tm, HEAD_PAD), lambda bi, i: (i, 0)),
            pl.BlockSpec((tm, HEAD_PAD), lambda bi, i: (i, 0)),
        ],
        out_specs=[row(hp), row(hp),
                   pl.BlockSpec((1, MLA_HEADS * V_ROWS, tm), lambda bi, i: (bi, 0, i)),
                   row(S5_WIDTH)],
        out_shape=[
            jax.ShapeDtypeStruct((b, l, hp), BF16),
            jax.ShapeDtypeStruct((b, l, hp), BF16),
            jax.ShapeDtypeStruct((b, MLA_HEADS * V_ROWS, l), BF16),
            jax.ShapeDtypeStruct((b, l, S5_WIDTH), F32),
        ],
        compiler_params=_params("parallel", "parallel"),
        name="even_in",
    )(x, mod, w["w_in"], w["q_norm"], w["kv_norm"], w["wq"], w["wq_rot"], w["wk"], w["wv_t"], cos, sin)


ATT_TQ = 256
ATT_TK = 512


HEADS_PER_STEP = 2
BF16_SUBLANES = 16
V_ROWS = MLA_V + BF16_SUBLANES


def _attn_kernel(q_ref, *refs, kv_lens):
    o_ref, s_buf = refs[-2], refs[-1]
    kv_refs = refs[:-2]
    tq = q_ref.shape[1]
    heads = range(HEADS_PER_STEP)
    q_t = [q_ref[0, :, hh * HEAD_PAD:(hh + 1) * HEAD_PAD].astype(F32).T.astype(BF16) for hh in heads]

    def scores(kc):
        return [_dot(kc[:, hh * HEAD_PAD:(hh + 1) * HEAD_PAD], q_t[hh]) for hh in heads]

    def accumulate(s_all, vt, carry):
        new = []
        for hh in heads:
            m, acc = carry[hh]
            s = s_all[hh]
            m_new = jnp.maximum(m, jnp.max(s, axis=0, keepdims=True))
            p = jnp.exp2(s - m_new).astype(BF16)
            acc = jnp.exp2(m - m_new) * acc + _dot(vt[hh * V_ROWS:(hh + 1) * V_ROWS], p)
            new.append((m_new, acc))
        return tuple(new)

    tk = s_buf.shape[2]

    def fill(slot, src, start, size):
        kc = kv_refs[2 * src][0, pl.ds(start, size), :]
        for hh, s in enumerate(scores(kc)):
            s_buf[slot, hh, :size] = s

    def drain(slot, src, start, size, c):
        vt = kv_refs[2 * src + 1][0, :, pl.ds(start, size)]
        return accumulate([s_buf[slot, hh, :size] for hh in heads], vt, c)

    carry = tuple((jnp.full((1, tq), -jnp.inf, F32), jnp.zeros((V_ROWS, tq), F32)) for _ in heads)
    chunks = [(src, j0, min(tk, n - j0)) for src, n in enumerate(kv_lens) for j0 in range(0, n, tk)]
    n_main = kv_lens[0] // tk if (kv_lens[0] % (2 * tk) == 0 and kv_lens[0] >= 4 * tk) else 0
    if n_main:
        chunks = chunks[n_main:]
        at = lambda j: pl.multiple_of(j * tk, tk)

        def body(j2, c):
            j = 2 * j2
            fill(1, 0, at(j + 1), tk)
            c = drain(0, 0, at(j), tk, c)
            fill(0, 0, at(j + 2), tk)
            return drain(1, 0, at(j + 1), tk, c)

        fill(0, 0, 0, tk)
        carry = lax.fori_loop(0, n_main // 2 - 1, body, carry)
        fill(1, 0, (n_main - 1) * tk, tk)
        carry = drain(0, 0, (n_main - 2) * tk, tk, carry)
        pending = (1, (0, (n_main - 1) * tk, tk))
    else:
        fill(0, *chunks[0])
        pending = (0, chunks[0])
        chunks = chunks[1:]
    for ch in chunks:
        slot = 1 - pending[0]
        fill(slot, *ch)
        carry = drain(pending[0], *pending[1], carry)
        pending = (slot, ch)
    carry = drain(pending[0], *pending[1], carry)
    out_t = jnp.concatenate([acc[:MLA_V] / acc[MLA_V:MLA_V + 1] for _, acc in carry], axis=0)
    o_ref[0] = out_t.T.astype(BF16)


def _attention(q, kvs):
    b, lq, _ = q.shape
    tq = min(ATT_TQ, lq)
    hs = HEADS_PER_STEP
    in_specs = [pl.BlockSpec((1, tq, hs * HEAD_PAD), lambda bi, hp, i: (bi, i, hp))]
    args = [q]
    for k, vt in kvs:
        n = k.shape[1]
        in_specs.append(pl.BlockSpec((1, n, hs * HEAD_PAD), lambda bi, hp, i: (bi, 0, hp)))
        in_specs.append(pl.BlockSpec((1, hs * V_ROWS, n), lambda bi, hp, i: (bi, hp, 0)))
        args += [k, vt]
    return pl.pallas_call(
        functools.partial(_attn_kernel, kv_lens=tuple(k.shape[1] for k, _ in kvs)),
        grid=(b, MLA_HEADS // hs, lq // tq),
        in_specs=in_specs,
        out_specs=pl.BlockSpec((1, tq, hs * MLA_V), lambda bi, hp, i: (bi, i, hp)),
        out_shape=jax.ShapeDtypeStruct((b, lq, MLA_HEADS * MLA_V), BF16),
        scratch_shapes=[pltpu.VMEM((2, hs, ATT_TK, tq), F32)],
        compiler_params=_params("parallel", "parallel", "parallel"),
        name="attention",
    )(*args)


def _s5_x_kernel(u_ref, w_ref, x_ref):
    x_ref[...] = _dot(u_ref[0], w_ref[0])


def _s5_chunk_inputs(u, wx):
    g, r, n = u.shape
    return pl.pallas_call(
        _s5_x_kernel,
        grid=(g,),
        in_specs=[pl.BlockSpec((1, r, n), lambda gi: (gi, 0, 0)),
                  pl.BlockSpec((1, n, n), lambda gi: (gi, 0, 0))],
        out_specs=pl.BlockSpec((r, n), lambda gi: (0, gi)),
        out_shape=jax.ShapeDtypeStruct((r, g * n), F32),
        compiler_params=_params("parallel"),
        name="s5_chunk_inputs",
    )(u, wx)


S5_SCAN_GROUPS = 16


def _s5_scan_kernel(x_ref, lre_ref, lim_ref, s_ref, *, n_ctx):
    n_tot = x_ref.shape[1]
    ns = S5_STATE
    lre = lre_ref[...]
    lim = lim_ref[...]
    is_fwd = lax.broadcasted_iota(jnp.int32, lre.shape, 1) < ns

    def step(cf, cb, carry):
        sre, sim = carry
        s_ref[0, cf, :, 0:ns] = sre[:, :ns]
        s_ref[0, cb, :, ns:2 * ns] = sre[:, ns:]
        s_ref[0, cf, :, 2 * ns:3 * ns] = sim[:, :ns]
        s_ref[0, cb, :, 3 * ns:] = sim[:, ns:]
        xf = x_ref[0, cf]
        xb = x_ref[0, cb]
        xre = jnp.where(is_fwd, xf[:, :2 * ns], xb[:, :2 * ns])
        xim = jnp.where(is_fwd, xf[:, 2 * ns:], xb[:, 2 * ns:])
        return sre * lre - sim * lim + xre, sre * lim + sim * lre + xim

    zero = jnp.zeros(lre.shape, F32)
    carry = lax.fori_loop(0, n_ctx, lambda i, c: step(i, n_ctx - 1 - i, c), (zero, zero))
    lax.fori_loop(n_ctx, n_tot, lambda i, c: step(i, n_tot + n_ctx - 1 - i, c), carry)


def _s5_scan(x, p1, p2, n_ctx):
    b, n, g, w = x.shape
    gb = S5_SCAN_GROUPS
    return pl.pallas_call(
        functools.partial(_s5_scan_kernel, n_ctx=n_ctx),
        grid=(b, g // gb),
        in_specs=[pl.BlockSpec((1, n, gb, w), lambda bi, gi: (bi, 0, gi, 0)),
                  pl.BlockSpec((gb, w // 2), lambda bi, gi: (gi, 0)),
                  pl.BlockSpec((gb, w // 2), lambda bi, gi: (gi, 0))],
        out_specs=pl.BlockSpec((1, n, gb, w), lambda bi, gi: (bi, 0, gi, 0)),
        out_shape=jax.ShapeDtypeStruct(x.shape, F32),
        compiler_params=_params("parallel", "parallel"),
        name="s5_scan",
    )(x, p1, p2)


def _s5_y_kernel(u_ref, s_ref, w_ref, y_ref):
    n = u_ref.shape[2]
    y_ref[0] = _dot(u_ref[0], w_ref[0, :n]) + _dot(s_ref[...].astype(BF16), w_ref[0, n:])


def _s5_outputs(u, s, wy):
    g, r, n = u.shape
    return pl.pallas_call(
        _s5_y_kernel,
        grid=(g,),
        in_specs=[pl.BlockSpec((1, r, n), lambda gi: (gi, 0, 0)),
                  pl.BlockSpec((r, n), lambda gi: (0, gi)),
                  pl.BlockSpec((1, 2 * n, n), lambda gi: (gi, 0, 0))],
        out_specs=pl.BlockSpec((1, r, n), lambda gi: (gi, 0, 0)),
        out_shape=jax.ShapeDtypeStruct((g, r, n), F32),
        compiler_params=_params("parallel"),
        name="s5_outputs",
    )(u, s, wy)


def _cmul(ar, ai, br, bi):
    return ar * br - ai * bi, ar * bi + ai * br


def _s5_weights(lam_re, lam_im, log_dt, b_re, b_im, c_re, c_im):
    t = S5_CHUNK
    hi = lax.Precision.HIGHEST
    dt = jnp.exp(log_dt)[..., None]
    mag = jnp.exp(lam_re * dt)
    lb_re = mag * jnp.cos(lam_im * dt)
    lb_im = mag * jnp.sin(lam_im * dt)
    den = lam_re * lam_re + lam_im * lam_im
    nr = lb_re - 1.0
    fr = (nr * lam_re + lb_im * lam_im) / den
    fi = (lb_im * lam_re - nr * lam_im) / den
    bb_re = fr[..., None] * b_re - fi[..., None] * b_im
    bb_im = fr[..., None] * b_im + fi[..., None] * b_re
    pw = [(jnp.ones_like(lb_re), jnp.zeros_like(lb_im))]
    for _ in range(t):
        pw.append(_cmul(pw[-1][0], pw[-1][1], lb_re, lb_im))
    pw_re = jnp.stack([p[0] for p in pw])
    pw_im = jnp.stack([p[1] for p in pw])
    pb_re, pb_im = _cmul(pw_re[:t, ..., None], pw_im[:t, ..., None], bb_re[None], bb_im[None])
    kern = (jnp.einsum('dgpn,ldgnq->ldgpq', c_re, pb_re, precision=hi)
            - jnp.einsum('dgpn,ldgnq->ldgpq', c_im, pb_im, precision=hi))
    s_idx = jnp.arange(t)[:, None]
    t_idx = jnp.arange(t)[None, :]
    lag = t_idx - s_idx
    kf = jnp.where((lag >= 0)[..., None, None, None], kern[jnp.clip(lag, 0, t - 1), 0], 0.0)
    kb = jnp.where((lag <= 0)[..., None, None, None], kern[jnp.clip(-lag, 0, t - 1), 1], 0.0)
    g = lam_re.shape[1]
    n_lane = t * S5_GROUP
    toep = (kf + kb).transpose(2, 0, 4, 1, 3).reshape(g, n_lane, n_lane)

    def to_state(w):
        return w.transpose(1, 0, 3, 2).reshape(g, n_lane, S5_STATE)

    wx = jnp.concatenate([to_state(pb_re[::-1, 0]), to_state(pb_re[:, 1]),
                          to_state(pb_im[::-1, 0]), to_state(pb_im[:, 1])], axis=-1)

    def from_state(d, p_re, p_im):
        cr = c_re[d][None]
        ci = c_im[d][None]
        a = cr * p_re[:, :, None, :] - ci * p_im[:, :, None, :]
        bneg = -(cr * p_im[:, :, None, :] + ci * p_re[:, :, None, :])
        lanes = lambda w: w.transpose(1, 3, 0, 2).reshape(g, S5_STATE, n_lane)
        return lanes(a), lanes(bneg)

    f_re, f_im = from_state(0, pw_re[1:, 0], pw_im[1:, 0])
    b_re, b_im = from_state(1, pw_re[t:0:-1, 1], pw_im[t:0:-1, 1])
    wy = jnp.concatenate([toep, f_re, b_re, f_im, b_im], axis=1)
    lt_re, lt_im = pw_re[t], pw_im[t]
    p1 = jnp.concatenate([lt_re[0], lt_re[1]], axis=-1)
    p2 = jnp.concatenate([lt_im[0], lt_im[1]], axis=-1)
    return wx.astype(BF16), wy.astype(BF16), p1, p2


def _s5_layer(us_ctx, us_lat, w):
    b, lc, _ = us_ctx.shape
    l = us_lat.shape[1]
    t, g, p = S5_CHUNK, S5_GROUPS, S5_GROUP

    def to_groups(u):
        n = u.shape[1]
        return u.astype(BF16).reshape(b, n // t, t, g, p).transpose(3, 0, 1, 2, 4).reshape(g, b, n // t, t * p)

    n_ctx, n_lat = lc // t, l // t
    n_tot = n_ctx + n_lat
    u = jnp.concatenate([to_groups(us_ctx), to_groups(us_lat)], axis=2).reshape(g, b * n_tot, t * p)
    x = _s5_chunk_inputs(u, w["wx"])
    s = _s5_scan(x.reshape(b, n_tot, g, 2 * LANE), w["p1"], w["p2"], n_ctx)
    y = _s5_outputs(u, s.reshape(b * n_tot, g * 2 * LANE), w["wy"])
    y = y.reshape(g, b, n_tot, t, p)

    def from_groups(yg):
        return yg.transpose(1, 2, 3, 0, 4).reshape(b, yg.shape[2] * t, g * p)

    return from_groups(y[:, :, :n_ctx]), from_groups(y[:, :, n_ctx:])


FFN_CHUNK = FFN_HIDDEN // 2


def _mid_tail(x, y, m, wffn_ref, lng_ref, lnb_ref, x1_ref, a_ref, gt_ref):
    x1 = _layer_norm(DN_ALPHA * x + m[2:3] * y, lng_ref[...], lnb_ref[...])
    x1_ref[0] = x1
    uf = (x1 * (1.0 + m[4:5]) + m[3:4]).astype(BF16)
    for j in range(FFN_HIDDEN // FFN_CHUNK):
        sl = slice(j * FFN_CHUNK, (j + 1) * FFN_CHUNK)
        a_ref[0, :, sl] = _dot(uf, wffn_ref[:, sl]).astype(BF16)
        gsl = slice(FFN_HIDDEN + j * FFN_CHUNK, FFN_HIDDEN + (j + 1) * FFN_CHUNK)
        gt_ref[0, :, sl] = _dot(uf, wffn_ref[:, gsl]).astype(BF16)


def _gelu_tanh(x):
    return 0.5 * x * (1.0 + jnp.tanh(math.sqrt(2.0 / math.pi) * (x + 0.044715 * x * x * x)))


def _even_mid_kernel(x_ref, mod_ref, att_ref, ys_ref, us_ref, d_ref, wglu_ref, wout_ref, lng_ref, lnb_ref,
                     wffn_ref, x1_ref, a_ref, gt_ref):
    z = _gelu_tanh(ys_ref[0] + d_ref[...] * us_ref[0])
    s5 = z * _sigmoid(_dot(z.astype(BF16), wglu_ref[...]))
    n_att = att_ref.shape[2]
    y = _dot(att_ref[0], wout_ref[:n_att]) + _dot(s5.astype(BF16), wout_ref[n_att:])
    _mid_tail(x_ref[0], y, mod_ref[0], wffn_ref, lng_ref, lnb_ref, x1_ref, a_ref, gt_ref)


def _odd_mid_kernel(x_ref, mod_ref, of_ref, ob_ref, g_ref, hn_ref, wout_ref, lng_ref, lnb_ref,
                    wffn_ref, x1_ref, a_ref, gt_ref):
    o = of_ref[0] + ob_ref[0]
    g = g_ref[0]
    gate = g * _sigmoid(g)
    parts = []
    for hh in range(HG_HEADS):
        sl = slice(hh * HG_DV, (hh + 1) * HG_DV)
        parts.append((_rms(o[:, sl], hn_ref[...]) * gate[:, sl]).astype(BF16))
    y = _dot(jnp.concatenate(parts, axis=-1), wout_ref[...])
    _mid_tail(x_ref[0], y, mod_ref[0], wffn_ref, lng_ref, lnb_ref, x1_ref, a_ref, gt_ref)


MID_TM = 256


def _mid_call(kernel, name, x, mod, row_args, row_specs, const_args):
    b, l, d = x.shape
    tm = min(MID_TM, l)
    row = lambda n: pl.BlockSpec((1, tm, n), lambda bi, i: (bi, i, 0))
    in_specs = [row(d), pl.BlockSpec((1, 8, d), lambda bi, i: (bi, 0, 0))]
    in_specs += [spec(tm) for spec in row_specs]
    in_specs += [_const_spec(a.shape) for a in const_args]
    return pl.pallas_call(
        kernel,
        grid=(b, l // tm),
        in_specs=in_specs,
        out_specs=[row(d), row(FFN_HIDDEN), row(FFN_HIDDEN)],
        out_shape=[jax.ShapeDtypeStruct((b, l, d), F32),
                   jax.ShapeDtypeStruct((b, l, FFN_HIDDEN), BF16),
                   jax.ShapeDtypeStruct((b, l, FFN_HIDDEN), BF16)],
        compiler_params=_params("parallel", "parallel"),
        name=name,
    )(x, mod, *row_args, *const_args)


def _row_spec(n, lane_block=0):
    return lambda tm: pl.BlockSpec((1, tm, n), lambda bi, i: (bi, i, lane_block))


def _even_mid(x, mod, att, ys, us, w):
    return _mid_call(
        _even_mid_kernel, "even_mid", x, mod, [att, ys, us],
        [_row_spec(att.shape[2]), _row_spec(S5_WIDTH), _row_spec(S5_WIDTH)],
        [w["s5_d"], w["w_glu"], w["w_out"], w["ln_g0"], w["ln_b0"], w["ffn_w_in"]])


def _odd_mid(x, mod, o_f, o_b, hg, w):
    return _mid_call(
        _odd_mid_kernel, "odd_mid", x, mod, [o_f, o_b, hg],
        [_row_spec(HG_WIDTH), _row_spec(HG_WIDTH), _row_spec(HG_WIDTH, 4)],
        [w["hg_norm"], w["w_out"], w["ln_g0"], w["ln_b0"], w["ffn_w_in"]])


FFN_TM = 256
HALO = 8


def _ffn_out_kernel(a_ref, ap_ref, an_ref, gt_ref, x1_ref, mod_ref, cw_ref, cb_ref, wout_ref, lng_ref, lnb_ref,
                    o_ref):
    i = pl.program_id(1)
    tm = a_ref.shape[1]
    a = a_ref[0].astype(F32)
    row = lax.broadcasted_iota(jnp.int32, (tm, 1), 0)
    before = jnp.where(i == 0, 0.0, ap_ref[0, HALO - 1:HALO, :].astype(F32))
    after = jnp.where(i == pl.num_programs(1) - 1, 0.0, an_ref[0, 0:1, :].astype(F32))
    a_prev = jnp.where(row == 0, before, pltpu.roll(a, 1, 0))
    a_next = jnp.where(row == tm - 1, after, pltpu.roll(a, tm - 1, 0))
    cw = cw_ref[...]
    conv = cb_ref[...] + cw[0:1] * a_prev + cw[1:2] * a + cw[2:3] * a_next
    hidden = (_silu(conv) * gt_ref[0].astype(F32)).astype(BF16)
    f = _dot(hidden, wout_ref[...])
    m = mod_ref[0]
    o_ref[0] = _layer_norm(DN_ALPHA * x1_ref[0] + m[5:6] * f, lng_ref[...], lnb_ref[...])


def _ffn_out(a, gt, x1, mod, w):
    b, l, d = x1.shape
    tm = min(FFN_TM, l)
    per = tm // HALO
    last = l // HALO - 1
    row = lambda n: pl.BlockSpec((1, tm, n), lambda bi, i: (bi, i, 0))
    consts = [w["conv_w"], w["conv_b"], w["ffn_w_out"], w["ln_g1"], w["ln_b1"]]
    return pl.pallas_call(
        _ffn_out_kernel,
        grid=(b, l // tm),
        in_specs=[
            row(FFN_HIDDEN),
            pl.BlockSpec((1, HALO, FFN_HIDDEN), lambda bi, i: (bi, jnp.maximum(i * per - 1, 0), 0)),
            pl.BlockSpec((1, HALO, FFN_HIDDEN), lambda bi, i: (bi, jnp.minimum((i + 1) * per, last), 0)),
            row(FFN_HIDDEN), row(d),
            pl.BlockSpec((1, 8, d), lambda bi, i: (bi, 0, 0)),
        ] + [_const_spec(c.shape) for c in consts],
        out_specs=row(d),
        out_shape=jax.ShapeDtypeStruct((b, l, d), F32),
        compiler_params=_params("parallel", "parallel"),
        name="ffn_out",
    )(a, a, a, gt, x1, mod, *consts)


ODD_TM = 256


def _odd_in_kernel(x_ref, mod_ref, w_ref, o_ref):
    m = mod_ref[0]
    u = (x_ref[0] * (1.0 + m[1:2]) + m[0:1]).astype(BF16)
    for j in range(HG_IN // HG_WIDTH):
        sl = slice(j * HG_WIDTH, (j + 1) * HG_WIDTH)
        o_ref[0, :, sl] = _dot(u, w_ref[:, sl])


def _odd_in(x, mod, w_in):
    b, l, d = x.shape
    tm = min(ODD_TM, l)
    return pl.pallas_call(
        _odd_in_kernel,
        grid=(b, l // tm),
        in_specs=[pl.BlockSpec((1, tm, d), lambda bi, i: (bi, i, 0)),
                  pl.BlockSpec((1, 8, d), lambda bi, i: (bi, 0, 0)),
                  _const_spec(w_in.shape)],
        out_specs=pl.BlockSpec((1, tm, HG_IN), lambda bi, i: (bi, i, 0)),
        out_shape=jax.ShapeDtypeStruct((b, l, HG_IN), F32),
        compiler_params=_params("parallel", "parallel"),
        name="odd_in",
    )(x, mod, w_in)


def _hgrn_direction(q, pre, v, lb, state, level, reverse):
    c = q.shape[0]
    row = lax.broadcasted_iota(jnp.int32, q.shape, 0)
    pos = (c - 1 - row) if reverse else row

    def prev(x, k):
        return pltpu.roll(x, (c - k) if reverse else k, 0)

    def nxt(x, k):
        return pltpu.roll(x, k if reverse else (c - k), 0)

    f = lb + (1.0 - lb) * _sigmoid(pre)
    key = 1.0 - f
    cum = jnp.log(f)
    k = 1
    while k < c:
        cum = cum + jnp.where(pos >= k, prev(cum, k), 0.0)
        k *= 2
    last = cum[0:1] if reverse else cum[c - 1:c]

    qb = q.astype(BF16)
    kb = key.astype(BF16)
    scores = jnp.where(level == 0, _dot_nt(qb, kb), 0.0)
    z = cum
    half = 1
    lvl = 1
    while half < c:
        ref = jnp.where((pos & half) == 0, z, prev(z, half))
        e = jnp.exp(-jnp.abs(cum - ref))
        x = (jnp.where((pos & half) != 0, q, key) * e).astype(BF16)
        scores = jnp.where(level == lvl, _dot_nt(x, x), scores)
        z = jnp.where((pos & half) != 0, z, nxt(z, half))
        half *= 2
        lvl += 1

    o = _dot(scores.astype(BF16), v)
    o = o + _dot_nt((q * jnp.exp(cum)).astype(BF16), state.astype(BF16))
    kd = (key * jnp.exp(last - cum)).astype(BF16)
    new_state = jnp.exp(last) * state + _dot_tn(v, kd)
    return o, new_state


def _hgrn_kernel(qf_ref, pf_ref, vf_ref, qb_ref, pb_ref, vb_ref, lb_ref, lvf_ref, lvb_ref, s0_ref,
                 of_ref, ob_ref, sfin_ref, st_ref):
    ci = pl.program_id(2)

    @pl.when(ci == 0)
    def _():
        st_ref[...] = s0_ref[0, 0]

    o, s = _hgrn_direction(qf_ref[0], pf_ref[0], vf_ref[0].astype(BF16), lb_ref[0], st_ref[0],
                           lvf_ref[...], False)
    of_ref[0] = o
    st_ref[0] = s
    o, s = _hgrn_direction(qb_ref[0], pb_ref[0], vb_ref[0].astype(BF16), lb_ref[1], st_ref[1],
                           lvb_ref[...], True)
    ob_ref[0] = o
    st_ref[1] = s

    @pl.when(ci == pl.num_programs(2) - 1)
    def _():
        sfin_ref[0, 0] = st_ref[...]


def _hgrn_levels(c):
    p = jnp.arange(c, dtype=jnp.int32)
    x = p[:, None] ^ p[None, :]
    lvl = jnp.where(x == 0, 0, 32 - lax.clz(x))
    fwd = jnp.where(p[:, None] >= p[None, :], lvl, -1)
    bwd = jnp.where(p[:, None] <= p[None, :], lvl, -1)
    return fwd.astype(jnp.int32), bwd.astype(jnp.int32)


def _hgrn(hg, lb, s0):
    b, l, _ = hg.shape
    c = HG_CHUNK
    n = l // c
    nh = HG_HEADS
    lvf, lvb = _hgrn_levels(c)
    fw = lambda blk: pl.BlockSpec((1, c, HG_DK), lambda bi, h, ci: (bi, ci, blk * nh + h))
    bw = lambda blk: pl.BlockSpec((1, c, HG_DK), lambda bi, h, ci: (bi, n - 1 - ci, blk * nh + h))
    state_spec = pl.BlockSpec((1, 1, 2, HG_DV, HG_DK), lambda bi, h, ci: (bi, h, 0, 0, 0))
    return pl.pallas_call(
        _hgrn_kernel,
        grid=(b, nh, n),
        in_specs=[fw(0), fw(1), fw(3), bw(0), bw(2), bw(3),
                  pl.BlockSpec((2, 1, HG_DK), lambda bi, h, ci: (0, 0, h)),
                  pl.BlockSpec((c, c), lambda bi, h, ci: (0, 0)),
                  pl.BlockSpec((c, c), lambda bi, h, ci: (0, 0)),
                  state_spec],
        out_specs=[pl.BlockSpec((1, c, HG_DV), lambda bi, h, ci: (bi, ci, h)),
                   pl.BlockSpec((1, c, HG_DV), lambda bi, h, ci: (bi, n - 1 - ci, h)),
                   state_spec],
        out_shape=[jax.ShapeDtypeStruct((b, l, HG_WIDTH), F32),
                   jax.ShapeDtypeStruct((b, l, HG_WIDTH), F32),
                   jax.ShapeDtypeStruct((b, nh, 2, HG_DV, HG_DK), F32)],
        scratch_shapes=[pltpu.VMEM((2, HG_DV, HG_DK), F32)],
        compiler_params=_params("parallel", "parallel", "arbitrary"),
        name="hgrn",
    )(hg, hg, hg, hg, hg, hg, lb, lvf, lvb, s0)


def _rope_tables(length):
    rows = length // GRID_W
    row = jnp.repeat(jnp.arange(rows, dtype=F32), GRID_W)
    col = jnp.tile(jnp.arange(GRID_W, dtype=F32), rows)
    n_freq = MLA_ROPE // 4
    inv = ROPE_BASE ** (-jnp.arange(n_freq, dtype=F32) / n_freq)
    ar = row[:, None] * inv
    ac = col[:, None] * inv
    ang = jnp.concatenate([ar, ar, ac, ac], axis=-1)
    return jnp.cos(ang), jnp.sin(ang)


def _head_tables(cos, sin):
    n = cos.shape[0]
    pad = jnp.zeros((n, HEAD_PAD - MLA_NOPE - MLA_ROPE), F32)
    return (jnp.concatenate([jnp.ones((n, MLA_NOPE), F32), cos, pad], -1),
            jnp.concatenate([jnp.zeros((n, MLA_NOPE), F32), sin, pad], -1))


def _rotate_cols(w):
    ws = w.reshape(w.shape[:-1] + (2, 2, MLA_ROPE // 4))
    return jnp.stack([-ws[..., 1, :], ws[..., 0, :]], axis=-2).reshape(w.shape)


def _even_weights(ev_w_in, q_norm, w_uq, kv_norm, w_ukv):
    c0 = MLA_Q_LORA
    c1 = c0 + MLA_KV_LORA
    c2 = c1 + MLA_ROPE
    d = ev_w_in.shape[0]
    w_kr = ev_w_in[:, c1:c2]
    lo = jnp.zeros((d, MLA_NOPE), F32)
    hi = jnp.zeros((d, HEAD_PAD - MLA_NOPE - MLA_ROPE), F32)
    w_in = jnp.concatenate([ev_w_in[:, :c1], ev_w_in[:, c2:], lo, w_kr, hi, lo, _rotate_cols(w_kr), hi], axis=-1)
    wq = w_uq.reshape(c0, MLA_HEADS, MLA_NOPE + MLA_ROPE)
    zq = jnp.zeros((c0, MLA_HEADS, HEAD_PAD - MLA_NOPE - MLA_ROPE), F32)
    wq_pad = jnp.concatenate([wq, zq], -1).reshape(c0, MLA_HEADS * HEAD_PAD)
    wq_rot = jnp.concatenate([jnp.zeros((c0, MLA_HEADS, MLA_NOPE), F32), _rotate_cols(wq[..., MLA_NOPE:]), zq],
                             -1).reshape(c0, MLA_HEADS * HEAD_PAD)
    wkv = w_ukv.reshape(MLA_KV_LORA, MLA_HEADS, MLA_NOPE + MLA_V)
    wk = jnp.concatenate([wkv[..., :MLA_NOPE], jnp.zeros((MLA_KV_LORA, MLA_HEADS, HEAD_PAD - MLA_NOPE), F32)],
                         -1).reshape(MLA_KV_LORA, MLA_HEADS * HEAD_PAD)
    wv_t = wkv[..., MLA_NOPE:].reshape(MLA_KV_LORA, MLA_HEADS * MLA_V).T
    return {"w_in": w_in.astype(BF16), "q_norm": q_norm.reshape(1, -1), "kv_norm": kv_norm.reshape(1, -1),
            "wq": wq_pad.astype(BF16), "wq_rot": wq_rot.astype(BF16), "wk": wk.astype(BF16),
            "wv_t": wv_t.astype(BF16)}


def _layer_mod(mod, layer, b):
    d = D_MODEL
    rows = mod[layer, :, :6 * d].reshape(8, 6, d)
    pad = jnp.zeros((2, d), F32)
    lat = jnp.stack([jnp.concatenate([rows[bi], pad], 0) for bi in range(b)])
    ctx = jnp.broadcast_to(jnp.concatenate([rows[b], pad], 0)[None], (b, 8, d))
    return lat, ctx


def kernel(x, c, ctx, c_ctx, ada_w, ada_b, ln_g, ln_b, ffn_w_in, ffn_conv_w, ffn_conv_b, ffn_w_out, ev_w_in,
           mla_q_norm, mla_w_uq, mla_kv_norm, mla_w_ukv, s5_lam_re, s5_lam_im, s5_log_dt, s5_b_re, s5_b_im,
           s5_c_re, s5_c_im, s5_d, s5_w_glu, ev_w_out, hg_w_in, hg_lb, hg_norm, hg_w_out):
    b, l, d = x.shape
    lc = ctx.shape[1]
    assert b + 1 == N_COND and d == D_MODEL and l % 512 == 0 and lc % 256 == 0

    cond_t = jnp.concatenate([c, c_ctx[None], jnp.zeros((8 - N_COND, d), F32)], 0).T
    mod = _adaln(cond_t, ada_w, ada_b)

    def ffn_weights(layer):
        return {"ln_g0": ln_g[layer, 0].reshape(1, d), "ln_b0": ln_b[layer, 0].reshape(1, d),
                "ln_g1": ln_g[layer, 1].reshape(1, d), "ln_b1": ln_b[layer, 1].reshape(1, d),
                "ffn_w_in": ffn_w_in[layer].astype(BF16),
                "conv_w": jnp.concatenate([ffn_conv_w[layer], jnp.zeros((5, FFN_HIDDEN), F32)], 0),
                "conv_b": ffn_conv_b[layer].reshape(1, FFN_HIDDEN),
                "ffn_w_out": ffn_w_out[layer].astype(BF16)}

    mod_lat, mod_ctx = _layer_mod(mod, 0, b)
    we = _even_weights(ev_w_in[0], mla_q_norm[0], mla_w_uq[0], mla_kv_norm[0], mla_w_ukv[0])
    cos_l, sin_l = _head_tables(*_rope_tables(l))
    cos_c, sin_c = _head_tables(jnp.ones((lc, MLA_ROPE), F32), jnp.zeros((lc, MLA_ROPE), F32))
    q_l, k_l, v_l, us_l = _even_in(x, mod_lat, we, cos_l, sin_l, min(512, l))
    q_c, k_c, v_c, us_c = _even_in(ctx, mod_ctx, we, cos_c, sin_c, min(512, lc))
    att_l = _attention(q_l, [(k_l, v_l), (k_c, v_c)])
    att_c = _attention(q_c, [(k_c, v_c)])
    wx, wy, p1, p2 = _s5_weights(s5_lam_re[0], s5_lam_im[0], s5_log_dt[0], s5_b_re[0], s5_b_im[0],
                                 s5_c_re[0], s5_c_im[0])
    ys_c, ys_l = _s5_layer(us_c, us_l, {"wx": wx, "wy": wy, "p1": p1, "p2": p2})
    w0 = ffn_weights(0)
    w0.update({"s5_d": s5_d[0].reshape(1, S5_WIDTH), "w_glu": s5_w_glu[0].astype(BF16),
               "w_out": ev_w_out[0].astype(BF16)})
    x1, a, gt = _even_mid(x, mod_lat, att_l, ys_l, us_l, w0)
    x = _ffn_out(a, gt, x1, mod_lat, w0)
    c1, a, gt = _even_mid(ctx, mod_ctx, att_c, ys_c, us_c, w0)
    ctx = _ffn_out(a, gt, c1, mod_ctx, w0)

    mod_lat, mod_ctx = _layer_mod(mod, 1, b)
    sm = jax.nn.softmax(hg_lb, axis=0)
    lower = (jnp.cumsum(sm, axis=0) - sm[0])[1].reshape(2, 1, HG_WIDTH)
    w_hg = hg_w_in[0].astype(BF16)
    hg_c = _odd_in(ctx, mod_ctx, w_hg)
    hg_l = _odd_in(x, mod_lat, w_hg)
    zero_state = jnp.zeros((b, HG_HEADS, 2, HG_DV, HG_DK), F32)
    _, _, s_ctx = _hgrn(hg_c, lower, zero_state)
    o_f, o_b, _ = _hgrn(hg_l, lower, s_ctx)
    w1 = ffn_weights(1)
    w1.update({"hg_norm": hg_norm[0].reshape(1, HG_DV), "w_out": hg_w_out[0].astype(BF16)})
    x1, a, gt = _odd_mid(x, mod_lat, o_f, o_b, hg_l, w1)
    return _ffn_out(a, gt, x1, mod_lat, w1)
```

```python
import functools
import math

import jax
import jax.numpy as jnp
from jax import lax
from jax.experimental import pallas as pl
from jax.experimental.pallas import tpu as pltpu

F32 = jnp.float32
BF16 = jnp.bfloat16

D_MODEL = 1024
DEPTH = 2
GRID_W = 64
NORM_EPS = 1e-6
DN_ALPHA = (2.0 * DEPTH) ** 0.25

MLA_HEADS = 8
MLA_NOPE = 64
MLA_ROPE = 32
MLA_V = 64
MLA_Q_LORA = 384
MLA_KV_LORA = 256
MLA_SCALE = (MLA_NOPE + MLA_ROPE) ** -0.5
Q_SCALE = MLA_SCALE * math.log2(math.e)
ROPE_BASE = 10000.0

S5_WIDTH = 512
S5_GROUP = 16
S5_GROUPS = S5_WIDTH // S5_GROUP
S5_STATE = 64
S5_CHUNK = 16

HG_HEADS = 8
HG_DK = 128
HG_DV = 128
HG_WIDTH = HG_HEADS * HG_DK
HG_IN = 5 * HG_WIDTH
HG_CHUNK = 128

FFN_HIDDEN = 2816

LANE = 128
HEAD_PAD = 128
EVEN_IN_PAD = MLA_Q_LORA + MLA_KV_LORA + S5_WIDTH + 2 * LANE

VMEM_LIMIT = 56 * 1024 * 1024


def _params(*sem):
    return pltpu.CompilerParams(dimension_semantics=sem, vmem_limit_bytes=VMEM_LIMIT)


def _const_spec(shape):
    zeros = (0,) * len(shape)
    return pl.BlockSpec(shape, lambda *_: zeros, pipeline_mode=pl.Buffered(1))


def _dot(a, b):
    return jnp.dot(a, b, preferred_element_type=F32)


def _dot_nt(a, b):
    return lax.dot_general(a, b, (((1,), (1,)), ((), ())), preferred_element_type=F32)


def _dot_tn(a, b):
    return lax.dot_general(a, b, (((0,), (0,)), ((), ())), preferred_element_type=F32)


def _sigmoid(x):
    return 1.0 / (1.0 + jnp.exp(-x))


def _silu(x):
    return x * _sigmoid(x)


def _rms(x, g):
    return x * lax.rsqrt(jnp.mean(x * x, -1, keepdims=True) + NORM_EPS) * g


def _layer_norm(x, g, b):
    mu = jnp.mean(x, -1, keepdims=True)
    xc = x - mu
    var = jnp.mean(xc * xc, -1, keepdims=True)
    return xc * lax.rsqrt(var + NORM_EPS) * g + b


ADA_TN = 512
N_COND = 3


def _adaln_kernel(ct_ref, w_ref, b_ref, o_ref):
    s = _silu(ct_ref[...])
    w = w_ref[0]
    rows = [jnp.sum(s[:, r:r + 1] * w, axis=0, keepdims=True) for r in range(N_COND)]
    rows.append(jnp.zeros((8 - N_COND, w.shape[1]), F32))
    o_ref[0] = jnp.concatenate(rows, axis=0) + b_ref[0]


def _adaln(cond_t, ada_w, ada_b):
    depth, d, n = ada_w.shape
    return pl.pallas_call(
        _adaln_kernel,
        grid=(depth, n // ADA_TN),
        in_specs=[
            pl.BlockSpec((d, 8), lambda l, j: (0, 0)),
            pl.BlockSpec((1, d, ADA_TN), lambda l, j: (l, 0, j)),
            pl.BlockSpec((1, 1, ADA_TN), lambda l, j: (l, 0, j)),
        ],
        out_specs=pl.BlockSpec((1, 8, ADA_TN), lambda l, j: (l, 0, j)),
        out_shape=jax.ShapeDtypeStruct((depth, 8, n), F32),
        compiler_params=_params("arbitrary", "arbitrary"),
        name="adaln",
    )(cond_t, ada_w, ada_b.reshape(depth, 1, n))


def _even_in_kernel(x_ref, mod_ref, win_ref, qg_ref, kvg_ref, wq_ref, wqr_ref, wk_ref, wvt_ref,
                    cos_ref, sin_ref, q_ref, k_ref, vt_ref, us_ref):
    m = mod_ref[0]
    u = (x_ref[0] * (1.0 + m[1:2]) + m[0:1]).astype(BF16)
    h = _dot(u, win_ref[...])
    c0 = MLA_Q_LORA
    c1 = c0 + MLA_KV_LORA
    c2 = c1 + S5_WIDTH
    us_ref[0] = h[:, c1:c2].astype(BF16)
    cos = cos_ref[...]
    sin = sin_ref[...]
    kr = h[:, c2:c2 + LANE] * cos + h[:, c2 + LANE:c2 + 2 * LANE] * sin
    cqn = _rms(h[:, :c0], qg_ref[...]).astype(BF16)
    qa = _dot(cqn, wq_ref[...])
    qb = _dot(cqn, wqr_ref[...])
    kvn = _rms(h[:, c0:c1], kvg_ref[...]).astype(BF16)
    ka = _dot(kvn, wk_ref[...])
    vt = _dot_nt(wvt_ref[...], kvn).astype(BF16)
    ones = jnp.ones((V_ROWS - MLA_V, vt.shape[1]), BF16)
    for hh in range(MLA_HEADS):
        vt_ref[0, hh * V_ROWS:hh * V_ROWS + MLA_V] = vt[hh * MLA_V:(hh + 1) * MLA_V]
        vt_ref[0, hh * V_ROWS + MLA_V:(hh + 1) * V_ROWS] = ones
    for hh in range(MLA_HEADS):
        sl = slice(hh * HEAD_PAD, (hh + 1) * HEAD_PAD)
        q_ref[0, :, sl] = ((qa[:, sl] * cos + qb[:, sl] * sin) * Q_SCALE).astype(BF16)
        k_ref[0, :, sl] = (ka[:, sl] + kr).astype(BF16)


def _even_in(x, mod, w, cos, sin, tm):
    b, l, d = x.shape
    hp = MLA_HEADS * HEAD_PAD
    row = lambda n: pl.BlockSpec((1, tm, n), lambda bi, i: (bi, i, 0))
    return pl.pallas_call(
        _even_in_kernel,
        grid=(b, l // tm),
        in_specs=[
            row(d),
            pl.BlockSpec((1, 8, d), lambda bi, i: (bi, 0, 0)),
            _const_spec(w["w_in"].shape), _const_spec(w["q_norm"].shape), _const_spec(w["kv_norm"].shape),
            _const_spec(w["wq"].shape), _const_spec(w["wq_rot"].shape), _const_spec(w["wk"].shape),
            _const_spec(w["wv_t"].shape),
            pl.BlockSpec((tm, HEAD_PAD), lambda bi, i: (i, 0)),
            pl.BlockSpec((tm, HEAD_PAD), lambda bi, i: (i, 0)),
        ],
        out_specs=[row(hp), row(hp),
                   pl.BlockSpec((1, MLA_HEADS * V_ROWS, tm), lambda bi, i: (bi, 0, i)),
                   row(S5_WIDTH)],
        out_shape=[
            jax.ShapeDtypeStruct((b, l, hp), BF16),
            jax.ShapeDtypeStruct((b, l, hp), BF16),
            jax.ShapeDtypeStruct((b, MLA_HEADS * V_ROWS, l), BF16),
            jax.ShapeDtypeStruct((b, l, S5_WIDTH), BF16),
        ],
        compiler_params=_params("parallel", "parallel"),
        name="even_in",
    )(x, mod, w["w_in"], w["q_norm"], w["kv_norm"], w["wq"], w["wq_rot"], w["wk"], w["wv_t"], cos, sin)


ATT_TQ = 512
ATT_TK = 512
ATT_UNROLL = 4


HEADS_PER_STEP = 2
BF16_SUBLANES = 16
V_ROWS = MLA_V + BF16_SUBLANES


def _attn_kernel(q_ref, *refs, kv_lens):
    o_ref, s_buf = refs[-2:]
    kv_refs = refs[:-2]
    tq = q_ref.shape[1]
    tk = s_buf.shape[2]
    heads = range(HEADS_PER_STEP)
    q_t = [q_ref[0, :, hh * HEAD_PAD:(hh + 1) * HEAD_PAD].astype(F32).T.astype(BF16) for hh in heads]

    def fill(slot, src, start, size):
        kc = kv_refs[2 * src][0, pl.ds(start, size), :]
        tops = []
        for hh in heads:
            s = _dot(kc[:, hh * HEAD_PAD:(hh + 1) * HEAD_PAD], q_t[hh])
            s_buf[slot, hh, :size] = s
            tops.append(jnp.max(s, axis=0, keepdims=True))
        return tuple(tops)

    def drain(slot, src, start, size, tops, carry):
        vt = kv_refs[2 * src + 1][0, :, pl.ds(start, size)]
        new = []
        for hh in heads:
            m, acc = carry[hh]
            m_new = jnp.maximum(m, tops[hh])
            p = jnp.exp2(s_buf[slot, hh, :size] - m_new).astype(BF16)
            acc = jnp.exp2(m - m_new) * acc + _dot(vt[hh * V_ROWS:(hh + 1) * V_ROWS], p)
            new.append((m_new, acc))
        return tuple(new)

    carry = tuple((jnp.full((1, tq), -jnp.inf, F32), jnp.zeros((V_ROWS, tq), F32)) for _ in heads)
    chunks = [(src, j0, min(tk, n - j0)) for src, n in enumerate(kv_lens) for j0 in range(0, n, tk)]
    trips = (kv_lens[0] // tk - 1) // ATT_UNROLL if kv_lens[0] % tk == 0 else 0
    trips = trips if trips >= 2 else 0
    tops = fill(0, *chunks[0])
    if trips:
        main = lambda j: (0, pl.multiple_of(j * tk, tk), tk)

        def body(t, state):
            tops, c = state
            for i in range(ATT_UNROLL):
                j = t * ATT_UNROLL + i
                tops_next = fill((i + 1) % 2, *main(j + 1))
                c = drain(i % 2, *main(j), tops, c)
                tops = tops_next
            return tops, c

        tops, carry = lax.fori_loop(0, trips, body, (tops, carry))
    for j in range(trips * ATT_UNROLL, len(chunks)):
        tops_next = fill((j + 1) % 2, *chunks[j + 1]) if j + 1 < len(chunks) else None
        carry = drain(j % 2, *chunks[j], tops, carry)
        tops = tops_next
    out_t = jnp.concatenate([acc[:MLA_V] / acc[MLA_V:MLA_V + 1] for _, acc in carry], axis=0)
    o_ref[0] = out_t.T.astype(BF16)


def _attention(q, kvs):
    b, lq, _ = q.shape
    tq = min(ATT_TQ, lq)
    hs = HEADS_PER_STEP
    in_specs = [pl.BlockSpec((1, tq, hs * HEAD_PAD), lambda bi, hp, i: (bi, i, hp))]
    args = [q]
    for k, vt in kvs:
        n = k.shape[1]
        in_specs.append(pl.BlockSpec((1, n, hs * HEAD_PAD), lambda bi, hp, i: (bi, 0, hp)))
        in_specs.append(pl.BlockSpec((1, hs * V_ROWS, n), lambda bi, hp, i: (bi, hp, 0)))
        args += [k, vt]
    return pl.pallas_call(
        functools.partial(_attn_kernel, kv_lens=tuple(k.shape[1] for k, _ in kvs)),
        grid=(b, MLA_HEADS // hs, lq // tq),
        in_specs=in_specs,
        out_specs=pl.BlockSpec((1, tq, hs * MLA_V), lambda bi, hp, i: (bi, i, hp)),
        out_shape=jax.ShapeDtypeStruct((b, lq, MLA_HEADS * MLA_V), BF16),
        scratch_shapes=[pltpu.VMEM((2, hs, ATT_TK, tq), F32)],
        compiler_params=_params("parallel", "parallel", "parallel"),
        name="attention",
    )(*args)


def _s5_x_kernel(uc_ref, ul_ref, w_ref, xc_ref, xl_ref):
    xc_ref[...] = _dot(uc_ref[0], w_ref[0])
    xl_ref[...] = _dot(ul_ref[0], w_ref[0])


def _s5_chunk_inputs(u_ctx, u_lat, wx):
    g, _, n = u_lat.shape
    rows = lambda u: pl.BlockSpec((1, u.shape[1], n), lambda gi: (gi, 0, 0))
    cols = lambda u: pl.BlockSpec((u.shape[1], n), lambda gi: (0, gi))
    return pl.pallas_call(
        _s5_x_kernel,
        grid=(g,),
        in_specs=[rows(u_ctx), rows(u_lat), pl.BlockSpec((1, n, n), lambda gi: (gi, 0, 0))],
        out_specs=[cols(u_ctx), cols(u_lat)],
        out_shape=[jax.ShapeDtypeStruct((u.shape[1], g * n), F32) for u in (u_ctx, u_lat)],
        compiler_params=_params("parallel"),
        name="s5_chunk_inputs",
    )(u_ctx, u_lat, wx)


S5_SCAN_GROUPS = 16


def _s5_scan_kernel(xc_ref, xl_ref, lre_ref, lim_ref, sc_ref, sl_ref):
    ns = S5_STATE
    lre = lre_ref[...]
    lim = lim_ref[...]
    is_fwd = lax.broadcasted_iota(jnp.int32, lre.shape, 1) < ns

    def walk(x_ref, s_ref, carry):
        n = x_ref.shape[1]

        def step(i, carry):
            sre, sim = carry
            cf = i
            cb = n - 1 - i
            s_ref[0, cf, :, 0:ns] = sre[:, :ns]
            s_ref[0, cb, :, ns:2 * ns] = sre[:, ns:]
            s_ref[0, cf, :, 2 * ns:3 * ns] = sim[:, :ns]
            s_ref[0, cb, :, 3 * ns:] = sim[:, ns:]
            xf = x_ref[0, cf]
            xb = x_ref[0, cb]
            xre = jnp.where(is_fwd, xf[:, :2 * ns], xb[:, :2 * ns])
            xim = jnp.where(is_fwd, xf[:, 2 * ns:], xb[:, 2 * ns:])
            return sre * lre - sim * lim + xre, sre * lim + sim * lre + xim

        return lax.fori_loop(0, n, step, carry)

    zero = jnp.zeros(lre.shape, F32)
    walk(xl_ref, sl_ref, walk(xc_ref, sc_ref, (zero, zero)))


def _s5_scan(x_ctx, x_lat, p1, p2):
    b, _, g, w = x_lat.shape
    gb = S5_SCAN_GROUPS
    chunks = lambda x: pl.BlockSpec((1, x.shape[1], gb, w), lambda bi, gi: (bi, 0, gi, 0))
    pole = pl.BlockSpec((gb, w // 2), lambda bi, gi: (gi, 0))
    return pl.pallas_call(
        _s5_scan_kernel,
        grid=(b, g // gb),
        in_specs=[chunks(x_ctx), chunks(x_lat), pole, pole],
        out_specs=[chunks(x_ctx), chunks(x_lat)],
        out_shape=[jax.ShapeDtypeStruct(x.shape, F32) for x in (x_ctx, x_lat)],
        compiler_params=_params("parallel", "parallel"),
        name="s5_scan",
    )(x_ctx, x_lat, p1, p2)


def _s5_y_kernel(uc_ref, sc_ref, ul_ref, sl_ref, w_ref, yc_ref, yl_ref):
    n = ul_ref.shape[2]
    for u_ref, s_ref, y_ref in ((uc_ref, sc_ref, yc_ref), (ul_ref, sl_ref, yl_ref)):
        y = _dot(u_ref[0], w_ref[0, :n]) + _dot(s_ref[...].astype(BF16), w_ref[0, n:])
        y_ref[0] = y.astype(y_ref.dtype)


def _s5_outputs(u_ctx, s_ctx, u_lat, s_lat, wy):
    g, _, n = u_lat.shape
    rows = lambda u: pl.BlockSpec((1, u.shape[1], n), lambda gi: (gi, 0, 0))
    cols = lambda u: pl.BlockSpec((u.shape[1], n), lambda gi: (0, gi))
    return pl.pallas_call(
        _s5_y_kernel,
        grid=(g,),
        in_specs=[rows(u_ctx), cols(u_ctx), rows(u_lat), cols(u_lat),
                  pl.BlockSpec((1, 2 * n, n), lambda gi: (gi, 0, 0))],
        out_specs=[rows(u_ctx), rows(u_lat)],
        out_shape=[jax.ShapeDtypeStruct(u.shape, BF16) for u in (u_ctx, u_lat)],
        compiler_params=_params("parallel"),
        name="s5_outputs",
    )(u_ctx, s_ctx, u_lat, s_lat, wy)


def _cmul(ar, ai, br, bi):
    return ar * br - ai * bi, ar * bi + ai * br


def _s5_weights(lam_re, lam_im, log_dt, b_re, b_im, c_re, c_im):
    t = S5_CHUNK
    hi = lax.Precision.HIGHEST
    dt = jnp.exp(log_dt)[..., None]
    mag = jnp.exp(lam_re * dt)
    lb_re = mag * jnp.cos(lam_im * dt)
    lb_im = mag * jnp.sin(lam_im * dt)
    den = lam_re * lam_re + lam_im * lam_im
    nr = lb_re - 1.0
    fr = (nr * lam_re + lb_im * lam_im) / den
    fi = (lb_im * lam_re - nr * lam_im) / den
    bb_re = fr[..., None] * b_re - fi[..., None] * b_im
    bb_im = fr[..., None] * b_im + fi[..., None] * b_re
    pw = [(jnp.ones_like(lb_re), jnp.zeros_like(lb_im))]
    for _ in range(t):
        pw.append(_cmul(pw[-1][0], pw[-1][1], lb_re, lb_im))
    pw_re = jnp.stack([p[0] for p in pw])
    pw_im = jnp.stack([p[1] for p in pw])
    pb_re, pb_im = _cmul(pw_re[:t, ..., None], pw_im[:t, ..., None], bb_re[None], bb_im[None])
    kern = (jnp.einsum('dgpn,ldgnq->ldgpq', c_re, pb_re, precision=hi)
            - jnp.einsum('dgpn,ldgnq->ldgpq', c_im, pb_im, precision=hi))
    lag = jnp.arange(t)[None, :] - jnp.arange(t)[:, None]
    pick = lambda d: (d[..., None] == jnp.arange(t)).astype(F32)
    kf = jnp.einsum('stl,lgpq->stgpq', pick(lag), kern[:, 0], precision=hi)
    kb = jnp.einsum('stl,lgpq->stgpq', pick(-lag), kern[:, 1], precision=hi)
    g = lam_re.shape[1]
    n_lane = t * S5_GROUP
    toep = (kf + kb).transpose(2, 0, 4, 1, 3).reshape(g, n_lane, n_lane)

    def to_state(w):
        return w.transpose(1, 0, 3, 2).reshape(g, n_lane, S5_STATE)

    wx = jnp.concatenate([to_state(pb_re[::-1, 0]), to_state(pb_re[:, 1]),
                          to_state(pb_im[::-1, 0]), to_state(pb_im[:, 1])], axis=-1)

    def from_state(d, p_re, p_im):
        cr = c_re[d][None]
        ci = c_im[d][None]
        a = cr * p_re[:, :, None, :] - ci * p_im[:, :, None, :]
        bneg = -(cr * p_im[:, :, None, :] + ci * p_re[:, :, None, :])
        lanes = lambda w: w.transpose(1, 3, 0, 2).reshape(g, S5_STATE, n_lane)
        return lanes(a), lanes(bneg)

    f_re, f_im = from_state(0, pw_re[1:, 0], pw_im[1:, 0])
    b_re, b_im = from_state(1, pw_re[t:0:-1, 1], pw_im[t:0:-1, 1])
    wy = jnp.concatenate([toep, f_re, b_re, f_im, b_im], axis=1)
    lt_re, lt_im = pw_re[t], pw_im[t]
    p1 = jnp.concatenate([lt_re[0], lt_re[1]], axis=-1)
    p2 = jnp.concatenate([lt_im[0], lt_im[1]], axis=-1)
    return wx.astype(BF16), wy.astype(BF16), p1, p2


def _s5_layer(us_ctx, us_lat, w):
    b = us_ctx.shape[0]
    t, g, p = S5_CHUNK, S5_GROUPS, S5_GROUP

    def to_groups(u):
        n = u.shape[1]
        return u.reshape(b, n // t, t, g, p).transpose(3, 0, 1, 2, 4).reshape(g, b * (n // t), t * p)

    def from_groups(y):
        n = y.shape[1] // b * t
        return y.reshape(g, b, n // t, t, p).transpose(1, 2, 3, 0, 4).reshape(b, n, g * p)

    u_c, u_l = to_groups(us_ctx), to_groups(us_lat)
    x_c, x_l = _s5_chunk_inputs(u_c, u_l, w["wx"])
    as_chunks = lambda x: x.reshape(b, x.shape[0] // b, g, 2 * LANE)
    s_c, s_l = _s5_scan(as_chunks(x_c), as_chunks(x_l), w["p1"], w["p2"])
    y_c, y_l = _s5_outputs(u_c, s_c.reshape(x_c.shape), u_l, s_l.reshape(x_l.shape), w["wy"])
    return from_groups(y_c), from_groups(y_l)


FFN_CHUNK = FFN_HIDDEN // 2


def _mid_tail(x, y, m, wffn_ref, lng_ref, lnb_ref, x1_ref, a_ref, gt_ref):
    x1 = _layer_norm(DN_ALPHA * x + m[2:3] * y, lng_ref[...], lnb_ref[...])
    x1_ref[0] = x1
    uf = (x1 * (1.0 + m[4:5]) + m[3:4]).astype(BF16)
    for j in range(FFN_HIDDEN // FFN_CHUNK):
        sl = slice(j * FFN_CHUNK, (j + 1) * FFN_CHUNK)
        a_ref[0, :, sl] = _dot(uf, wffn_ref[:, sl]).astype(BF16)
        gsl = slice(FFN_HIDDEN + j * FFN_CHUNK, FFN_HIDDEN + (j + 1) * FFN_CHUNK)
        gt_ref[0, :, sl] = _dot(uf, wffn_ref[:, gsl]).astype(BF16)


def _gelu_tanh(x):
    return 0.5 * x * (1.0 + jnp.tanh(math.sqrt(2.0 / math.pi) * (x + 0.044715 * x * x * x)))


def _even_mid_kernel(x_ref, mod_ref, att_ref, ys_ref, us_ref, d_ref, wglu_ref, wout_ref, lng_ref, lnb_ref,
                     wffn_ref, x1_ref, a_ref, gt_ref):
    z = _gelu_tanh(ys_ref[0].astype(F32) + d_ref[...] * us_ref[0].astype(F32))
    s5 = z * _sigmoid(_dot(z.astype(BF16), wglu_ref[...]))
    n_att = att_ref.shape[2]
    y = _dot(att_ref[0], wout_ref[:n_att]) + _dot(s5.astype(BF16), wout_ref[n_att:])
    _mid_tail(x_ref[0], y, mod_ref[0], wffn_ref, lng_ref, lnb_ref, x1_ref, a_ref, gt_ref)


def _odd_mid_kernel(x_ref, mod_ref, of_ref, ob_ref, g_ref, hn_ref, wout_ref, lng_ref, lnb_ref,
                    wffn_ref, x1_ref, a_ref, gt_ref):
    o = of_ref[0] + ob_ref[0]
    g = g_ref[0]
    gate = g * _sigmoid(g)
    parts = []
    for hh in range(HG_HEADS):
        sl = slice(hh * HG_DV, (hh + 1) * HG_DV)
        parts.append((_rms(o[:, sl], hn_ref[...]) * gate[:, sl]).astype(BF16))
    y = _dot(jnp.concatenate(parts, axis=-1), wout_ref[...])
    _mid_tail(x_ref[0], y, mod_ref[0], wffn_ref, lng_ref, lnb_ref, x1_ref, a_ref, gt_ref)


MID_TM = 256


def _mid_call(kernel, name, x, mod, row_args, row_specs, const_args):
    b, l, d = x.shape
    tm = min(MID_TM, l)
    row = lambda n: pl.BlockSpec((1, tm, n), lambda bi, i: (bi, i, 0))
    in_specs = [row(d), pl.BlockSpec((1, 8, d), lambda bi, i: (bi, 0, 0))]
    in_specs += [spec(tm) for spec in row_specs]
    in_specs += [_const_spec(a.shape) for a in const_args]
    return pl.pallas_call(
        kernel,
        grid=(b, l // tm),
        in_specs=in_specs,
        out_specs=[row(d), row(FFN_HIDDEN), row(FFN_HIDDEN)],
        out_shape=[jax.ShapeDtypeStruct((b, l, d), F32),
                   jax.ShapeDtypeStruct((b, l, FFN_HIDDEN), BF16),
                   jax.ShapeDtypeStruct((b, l, FFN_HIDDEN), BF16)],
        compiler_params=_params("parallel", "parallel"),
        name=name,
    )(x, mod, *row_args, *const_args)


def _row_spec(n, lane_block=0):
    return lambda tm: pl.BlockSpec((1, tm, n), lambda bi, i: (bi, i, lane_block))


def _even_mid(x, mod, att, ys, us, w):
    return _mid_call(
        _even_mid_kernel, "even_mid", x, mod, [att, ys, us],
        [_row_spec(att.shape[2]), _row_spec(S5_WIDTH), _row_spec(S5_WIDTH)],
        [w["s5_d"], w["w_glu"], w["w_out"], w["ln_g0"], w["ln_b0"], w["ffn_w_in"]])


def _odd_mid(x, mod, o_f, o_b, hg, w):
    return _mid_call(
        _odd_mid_kernel, "odd_mid", x, mod, [o_f, o_b, hg],
        [_row_spec(HG_WIDTH), _row_spec(HG_WIDTH), _row_spec(HG_WIDTH, 4)],
        [w["hg_norm"], w["w_out"], w["ln_g0"], w["ln_b0"], w["ffn_w_in"]])


FFN_TM = 256
HALO = 8


def _ffn_out_kernel(a_ref, ap_ref, an_ref, gt_ref, x1_ref, mod_ref, cw_ref, cb_ref, wout_ref, lng_ref, lnb_ref,
                    o_ref):
    i = pl.program_id(1)
    tm = a_ref.shape[1]
    a = a_ref[0].astype(F32)
    row = lax.broadcasted_iota(jnp.int32, (tm, 1), 0)
    before = jnp.where(i == 0, 0.0, ap_ref[0, HALO - 1:HALO, :].astype(F32))
    after = jnp.where(i == pl.num_programs(1) - 1, 0.0, an_ref[0, 0:1, :].astype(F32))
    a_prev = jnp.where(row == 0, before, pltpu.roll(a, 1, 0))
    a_next = jnp.where(row == tm - 1, after, pltpu.roll(a, tm - 1, 0))
    cw = cw_ref[...]
    conv = cb_ref[...] + cw[0:1] * a_prev + cw[1:2] * a + cw[2:3] * a_next
    hidden = (_silu(conv) * gt_ref[0].astype(F32)).astype(BF16)
    f = _dot(hidden, wout_ref[...])
    m = mod_ref[0]
    o_ref[0] = _layer_norm(DN_ALPHA * x1_ref[0] + m[5:6] * f, lng_ref[...], lnb_ref[...])


def _ffn_out(a, gt, x1, mod, w):
    b, l, d = x1.shape
    tm = min(FFN_TM, l)
    per = tm // HALO
    last = l // HALO - 1
    row = lambda n: pl.BlockSpec((1, tm, n), lambda bi, i: (bi, i, 0))
    consts = [w["conv_w"], w["conv_b"], w["ffn_w_out"], w["ln_g1"], w["ln_b1"]]
    return pl.pallas_call(
        _ffn_out_kernel,
        grid=(b, l // tm),
        in_specs=[
            row(FFN_HIDDEN),
            pl.BlockSpec((1, HALO, FFN_HIDDEN), lambda bi, i: (bi, jnp.maximum(i * per - 1, 0), 0)),
            pl.BlockSpec((1, HALO, FFN_HIDDEN), lambda bi, i: (bi, jnp.minimum((i + 1) * per, last), 0)),
            row(FFN_HIDDEN), row(d),
            pl.BlockSpec((1, 8, d), lambda bi, i: (bi, 0, 0)),
        ] + [_const_spec(c.shape) for c in consts],
        out_specs=row(d),
        out_shape=jax.ShapeDtypeStruct((b, l, d), F32),
        compiler_params=_params("parallel", "parallel"),
        name="ffn_out",
    )(a, a, a, gt, x1, mod, *consts)


ODD_TM = 256


def _odd_in_kernel(x_ref, mod_ref, w_ref, o_ref):
    m = mod_ref[0]
    u = (x_ref[0] * (1.0 + m[1:2]) + m[0:1]).astype(BF16)
    for j in range(HG_IN // HG_WIDTH):
        sl = slice(j * HG_WIDTH, (j + 1) * HG_WIDTH)
        o_ref[0, :, sl] = _dot(u, w_ref[:, sl])


def _odd_in(x, mod, w_in):
    b, l, d = x.shape
    tm = min(ODD_TM, l)
    return pl.pallas_call(
        _odd_in_kernel,
        grid=(b, l // tm),
        in_specs=[pl.BlockSpec((1, tm, d), lambda bi, i: (bi, i, 0)),
                  pl.BlockSpec((1, 8, d), lambda bi, i: (bi, 0, 0)),
                  _const_spec(w_in.shape)],
        out_specs=pl.BlockSpec((1, tm, HG_IN), lambda bi, i: (bi, i, 0)),
        out_shape=jax.ShapeDtypeStruct((b, l, HG_IN), F32),
        compiler_params=_params("parallel", "parallel"),
        name="odd_in",
    )(x, mod, w_in)


def _hgrn_direction(q, pre, v, lb, state, level, reverse):
    c = q.shape[0]
    row = lax.broadcasted_iota(jnp.int32, q.shape, 0)
    pos = (c - 1 - row) if reverse else row

    def prev(x, k):
        return pltpu.roll(x, (c - k) if reverse else k, 0)

    def nxt(x, k):
        return pltpu.roll(x, k if reverse else (c - k), 0)

    f = lb + (1.0 - lb) * _sigmoid(pre)
    key = 1.0 - f
    cum = jnp.log(f)
    k = 1
    while k < c:
        cum = cum + jnp.where(pos >= k, prev(cum, k), 0.0)
        k *= 2
    last = cum[0:1] if reverse else cum[c - 1:c]

    qb = q.astype(BF16)
    kb = key.astype(BF16)
    scores = jnp.where(level == 0, _dot_nt(qb, kb), 0.0)
    z = cum
    half = 1
    lvl = 1
    while half < c:
        ref = jnp.where((pos & half) == 0, z, prev(z, half))
        e = jnp.exp(-jnp.abs(cum - ref))
        x = (jnp.where((pos & half) != 0, q, key) * e).astype(BF16)
        scores = jnp.where(level == lvl, _dot_nt(x, x), scores)
        z = jnp.where((pos & half) != 0, z, nxt(z, half))
        half *= 2
        lvl += 1

    o = _dot(scores.astype(BF16), v)
    o = o + _dot_nt((q * jnp.exp(cum)).astype(BF16), state.astype(BF16))
    kd = (key * jnp.exp(last - cum)).astype(BF16)
    new_state = jnp.exp(last) * state + _dot_tn(v, kd)
    return o, new_state


def _hgrn_kernel(qf_ref, pf_ref, vf_ref, qb_ref, pb_ref, vb_ref, lb_ref, lvf_ref, lvb_ref, s0_ref,
                 of_ref, ob_ref, sfin_ref, st_ref):
    ci = pl.program_id(2)

    @pl.when(ci == 0)
    def _():
        st_ref[...] = s0_ref[0, 0]

    o, s = _hgrn_direction(qf_ref[0], pf_ref[0], vf_ref[0].astype(BF16), lb_ref[0], st_ref[0],
                           lvf_ref[...], False)
    of_ref[0] = o
    st_ref[0] = s
    o, s = _hgrn_direction(qb_ref[0], pb_ref[0], vb_ref[0].astype(BF16), lb_ref[1], st_ref[1],
                           lvb_ref[...], True)
    ob_ref[0] = o
    st_ref[1] = s

    @pl.when(ci == pl.num_programs(2) - 1)
    def _():
        sfin_ref[0, 0] = st_ref[...]


def _hgrn_levels(c):
    p = jnp.arange(c, dtype=jnp.int32)
    x = p[:, None] ^ p[None, :]
    lvl = jnp.where(x == 0, 0, 32 - lax.clz(x))
    fwd = jnp.where(p[:, None] >= p[None, :], lvl, -1)
    bwd = jnp.where(p[:, None] <= p[None, :], lvl, -1)
    return fwd.astype(jnp.int32), bwd.astype(jnp.int32)


def _hgrn(hg, lb, s0):
    b, l, _ = hg.shape
    c = HG_CHUNK
    n = l // c
    nh = HG_HEADS
    lvf, lvb = _hgrn_levels(c)
    fw = lambda blk: pl.BlockSpec((1, c, HG_DK), lambda bi, h, ci: (bi, ci, blk * nh + h))
    bw = lambda blk: pl.BlockSpec((1, c, HG_DK), lambda bi, h, ci: (bi, n - 1 - ci, blk * nh + h))
    state_spec = pl.BlockSpec((1, 1, 2, HG_DV, HG_DK), lambda bi, h, ci: (bi, h, 0, 0, 0))
    return pl.pallas_call(
        _hgrn_kernel,
        grid=(b, nh, n),
        in_specs=[fw(0), fw(1), fw(3), bw(0), bw(2), bw(3),
                  pl.BlockSpec((2, 1, HG_DK), lambda bi, h, ci: (0, 0, h)),
                  pl.BlockSpec((c, c), lambda bi, h, ci: (0, 0)),
                  pl.BlockSpec((c, c), lambda bi, h, ci: (0, 0)),
                  state_spec],
        out_specs=[pl.BlockSpec((1, c, HG_DV), lambda bi, h, ci: (bi, ci, h)),
                   pl.BlockSpec((1, c, HG_DV), lambda bi, h, ci: (bi, n - 1 - ci, h)),
                   state_spec],
        out_shape=[jax.ShapeDtypeStruct((b, l, HG_WIDTH), F32),
                   jax.ShapeDtypeStruct((b, l, HG_WIDTH), F32),
                   jax.ShapeDtypeStruct((b, nh, 2, HG_DV, HG_DK), F32)],
        scratch_shapes=[pltpu.VMEM((2, HG_DV, HG_DK), F32)],
        compiler_params=_params("parallel", "parallel", "arbitrary"),
        name="hgrn",
    )(hg, hg, hg, hg, hg, hg, lb, lvf, lvb, s0)


HG_DIRECT_CHUNK = 64
HG_DIRECT_BLOCK = 512
HG_DIRECT_MAX_EXPONENT = 60.0
HG_DIRECT_MIN_LB = math.exp(-HG_DIRECT_MAX_EXPONENT / (HG_DIRECT_CHUNK // 2))


def _split3(x):
    hi = x.astype(BF16)
    r = x - hi.astype(F32)
    mid = r.astype(BF16)
    lo = (r - mid.astype(F32)).astype(BF16)
    return hi, mid, lo


def _hgrn_direct_block(q_ref, p_ref, v_ref, o_ref, lb, state, tri3, visible, reverse):
    c = HG_DIRECT_CHUNK
    n_sub = q_ref.shape[1] // c
    half = c // 2
    order = list(range(n_sub - 1, -1, -1) if reverse else range(n_sub))
    rows = [slice(j * c, (j + 1) * c) for j in order]
    fs = [lb + (1.0 - lb) * _sigmoid(p_ref[0, r, :]) for r in rows]
    cums = [_dot(tri3, jnp.concatenate(_split3(jnp.log(f)), axis=0)) for f in fs]
    mids = [cum[half:half + 1] if reverse else cum[half - 1:half] for cum in cums]
    lasts = [cum[0:1] if reverse else cum[c - 1:c] for cum in cums]
    qts = [(q_ref[0, r, :] * jnp.exp(cum - mid)).astype(BF16) for r, cum, mid in zip(rows, cums, mids)]
    kts = [((1.0 - f) * jnp.exp(mid - cum)).astype(BF16) for f, cum, mid in zip(fs, cums, mids)]
    vs = [v_ref[0, r, :].astype(BF16) for r in rows]
    scores = [jnp.where(visible, _dot_nt(qt, kt), 0.0).astype(BF16) for qt, kt in zip(qts, kts)]
    adds = [jnp.exp(last - mid) * _dot_tn(v, kt) for last, mid, v, kt in zip(lasts, mids, vs, kts)]
    intra = [_dot(sc, v) for sc, v in zip(scores, vs)]
    entering = []
    for mid, last, add in zip(mids, lasts, adds):
        entering.append((state * jnp.exp(mid)).astype(BF16))
        state = jnp.exp(last) * state + add
    for r, qt, o, s_in in zip(rows, qts, intra, entering):
        o_ref[0, r, :] = o + _dot_nt(qt, s_in)
    return state


def _hgrn_direct_kernel(qf_ref, pf_ref, vf_ref, qb_ref, pb_ref, vb_ref, lb_ref, trif_ref, trib_ref, s0_ref,
                        of_ref, ob_ref, sfin_ref, st_ref):
    ci = pl.program_id(2)
    c = HG_DIRECT_CHUNK

    @pl.when(ci == 0)
    def _():
        st_ref[...] = s0_ref[0, 0]

    row = lax.broadcasted_iota(jnp.int32, (c, c), 0)
    col = lax.broadcasted_iota(jnp.int32, (c, c), 1)
    st_ref[0] = _hgrn_direct_block(qf_ref, pf_ref, vf_ref, of_ref, lb_ref[0], st_ref[0], trif_ref[...],
                                   row >= col, False)
    st_ref[1] = _hgrn_direct_block(qb_ref, pb_ref, vb_ref, ob_ref, lb_ref[1], st_ref[1], trib_ref[...],
                                   row <= col, True)

    @pl.when(ci == pl.num_programs(2) - 1)
    def _():
        sfin_ref[0, 0] = st_ref[...]


def _hgrn_direct(hg, lb, s0):
    b, l, _ = hg.shape
    c = HG_DIRECT_CHUNK
    blk_rows = min(HG_DIRECT_BLOCK, l)
    n = l // blk_rows
    nh = HG_HEADS
    p = jnp.arange(c)
    lower_tri = (p[:, None] >= p[None, :]).astype(BF16)
    trif = jnp.concatenate([lower_tri] * 3, axis=1)
    trib = jnp.concatenate([lower_tri.T] * 3, axis=1)
    fw = lambda blk: pl.BlockSpec((1, blk_rows, HG_DK), lambda bi, h, ci: (bi, ci, blk * nh + h))
    bw = lambda blk: pl.BlockSpec((1, blk_rows, HG_DK), lambda bi, h, ci: (bi, n - 1 - ci, blk * nh + h))
    state_spec = pl.BlockSpec((1, 1, 2, HG_DV, HG_DK), lambda bi, h, ci: (bi, h, 0, 0, 0))
    return pl.pallas_call(
        _hgrn_direct_kernel,
        grid=(b, nh, n),
        in_specs=[fw(0), fw(1), fw(3), bw(0), bw(2), bw(3),
                  pl.BlockSpec((2, 1, HG_DK), lambda bi, h, ci: (0, 0, h)),
                  pl.BlockSpec((c, 3 * c), lambda bi, h, ci: (0, 0)),
                  pl.BlockSpec((c, 3 * c), lambda bi, h, ci: (0, 0)),
                  state_spec],
        out_specs=[pl.BlockSpec((1, blk_rows, HG_DV), lambda bi, h, ci: (bi, ci, h)),
                   pl.BlockSpec((1, blk_rows, HG_DV), lambda bi, h, ci: (bi, n - 1 - ci, h)),
                   state_spec],
        out_shape=[jax.ShapeDtypeStruct((b, l, HG_WIDTH), F32),
                   jax.ShapeDtypeStruct((b, l, HG_WIDTH), F32),
                   jax.ShapeDtypeStruct((b, nh, 2, HG_DV, HG_DK), F32)],
        scratch_shapes=[pltpu.VMEM((2, HG_DV, HG_DK), F32)],
        compiler_params=_params("parallel", "parallel", "arbitrary"),
        name="hgrn_direct",
    )(hg, hg, hg, hg, hg, hg, lb, trif, trib, s0)


def _hgrn_auto(hg, lb, s0):
    return lax.cond(jnp.min(lb) >= HG_DIRECT_MIN_LB,
                    lambda: tuple(_hgrn_direct(hg, lb, s0)), lambda: tuple(_hgrn(hg, lb, s0)))


def _rope_tables(length):
    rows = length // GRID_W
    row = jnp.repeat(jnp.arange(rows, dtype=F32), GRID_W)
    col = jnp.tile(jnp.arange(GRID_W, dtype=F32), rows)
    n_freq = MLA_ROPE // 4
    inv = ROPE_BASE ** (-jnp.arange(n_freq, dtype=F32) / n_freq)
    ar = row[:, None] * inv
    ac = col[:, None] * inv
    ang = jnp.concatenate([ar, ar, ac, ac], axis=-1)
    return jnp.cos(ang), jnp.sin(ang)


def _head_tables(cos, sin):
    n = cos.shape[0]
    pad = jnp.zeros((n, HEAD_PAD - MLA_NOPE - MLA_ROPE), F32)
    return (jnp.concatenate([jnp.ones((n, MLA_NOPE), F32), cos, pad], -1),
            jnp.concatenate([jnp.zeros((n, MLA_NOPE), F32), sin, pad], -1))


def _rotate_cols(w):
    ws = w.reshape(w.shape[:-1] + (2, 2, MLA_ROPE // 4))
    return jnp.stack([-ws[..., 1, :], ws[..., 0, :]], axis=-2).reshape(w.shape)


def _even_weights(ev_w_in, q_norm, w_uq, kv_norm, w_ukv):
    c0 = MLA_Q_LORA
    c1 = c0 + MLA_KV_LORA
    c2 = c1 + MLA_ROPE
    d = ev_w_in.shape[0]
    w_kr = ev_w_in[:, c1:c2]
    lo = jnp.zeros((d, MLA_NOPE), F32)
    hi = jnp.zeros((d, HEAD_PAD - MLA_NOPE - MLA_ROPE), F32)
    w_in = jnp.concatenate([ev_w_in[:, :c1], ev_w_in[:, c2:], lo, w_kr, hi, lo, _rotate_cols(w_kr), hi], axis=-1)
    wq = w_uq.reshape(c0, MLA_HEADS, MLA_NOPE + MLA_ROPE)
    zq = jnp.zeros((c0, MLA_HEADS, HEAD_PAD - MLA_NOPE - MLA_ROPE), F32)
    wq_pad = jnp.concatenate([wq, zq], -1).reshape(c0, MLA_HEADS * HEAD_PAD)
    wq_rot = jnp.concatenate([jnp.zeros((c0, MLA_HEADS, MLA_NOPE), F32), _rotate_cols(wq[..., MLA_NOPE:]), zq],
                             -1).reshape(c0, MLA_HEADS * HEAD_PAD)
    wkv = w_ukv.reshape(MLA_KV_LORA, MLA_HEADS, MLA_NOPE + MLA_V)
    wk = jnp.concatenate([wkv[..., :MLA_NOPE], jnp.zeros((MLA_KV_LORA, MLA_HEADS, HEAD_PAD - MLA_NOPE), F32)],
                         -1).reshape(MLA_KV_LORA, MLA_HEADS * HEAD_PAD)
    wv_t = wkv[..., MLA_NOPE:].reshape(MLA_KV_LORA, MLA_HEADS * MLA_V).T
    return {"w_in": w_in.astype(BF16), "q_norm": q_norm.reshape(1, -1), "kv_norm": kv_norm.reshape(1, -1),
            "wq": wq_pad.astype(BF16), "wq_rot": wq_rot.astype(BF16), "wk": wk.astype(BF16),
            "wv_t": wv_t.astype(BF16)}


def _layer_mod(mod, layer, b):
    d = D_MODEL
    rows = mod[layer, :, :6 * d].reshape(8, 6, d)
    pad = jnp.zeros((2, d), F32)
    lat = jnp.stack([jnp.concatenate([rows[bi], pad], 0) for bi in range(b)])
    ctx = jnp.broadcast_to(jnp.concatenate([rows[b], pad], 0)[None], (b, 8, d))
    return lat, ctx


def kernel(x, c, ctx, c_ctx, ada_w, ada_b, ln_g, ln_b, ffn_w_in, ffn_conv_w, ffn_conv_b, ffn_w_out, ev_w_in,
           mla_q_norm, mla_w_uq, mla_kv_norm, mla_w_ukv, s5_lam_re, s5_lam_im, s5_log_dt, s5_b_re, s5_b_im,
           s5_c_re, s5_c_im, s5_d, s5_w_glu, ev_w_out, hg_w_in, hg_lb, hg_norm, hg_w_out):
    b, l, d = x.shape
    lc = ctx.shape[1]
    assert b + 1 == N_COND and d == D_MODEL and l % 512 == 0 and lc % 256 == 0

    cond_t = jnp.concatenate([c, c_ctx[None], jnp.zeros((8 - N_COND, d), F32)], 0).T
    mod = _adaln(cond_t, ada_w, ada_b)

    def ffn_weights(layer):
        return {"ln_g0": ln_g[layer, 0].reshape(1, d), "ln_b0": ln_b[layer, 0].reshape(1, d),
                "ln_g1": ln_g[layer, 1].reshape(1, d), "ln_b1": ln_b[layer, 1].reshape(1, d),
                "ffn_w_in": ffn_w_in[layer].astype(BF16),
                "conv_w": jnp.concatenate([ffn_conv_w[layer], jnp.zeros((5, FFN_HIDDEN), F32)], 0),
                "conv_b": ffn_conv_b[layer].reshape(1, FFN_HIDDEN),
                "ffn_w_out": ffn_w_out[layer].astype(BF16)}

    mod_lat, mod_ctx = _layer_mod(mod, 0, b)
    we = _even_weights(ev_w_in[0], mla_q_norm[0], mla_w_uq[0], mla_kv_norm[0], mla_w_ukv[0])
    cos_l, sin_l = _head_tables(*_rope_tables(l))
    cos_c, sin_c = _head_tables(jnp.ones((lc, MLA_ROPE), F32), jnp.zeros((lc, MLA_ROPE), F32))
    q_l, k_l, v_l, us_l = _even_in(x, mod_lat, we, cos_l, sin_l, min(512, l))
    q_c, k_c, v_c, us_c = _even_in(ctx, mod_ctx, we, cos_c, sin_c, min(512, lc))
    att_l = _attention(q_l, [(k_l, v_l), (k_c, v_c)])
    att_c = _attention(q_c, [(k_c, v_c)])
    wx, wy, p1, p2 = _s5_weights(s5_lam_re[0], s5_lam_im[0], s5_log_dt[0], s5_b_re[0], s5_b_im[0],
                                 s5_c_re[0], s5_c_im[0])
    ys_c, ys_l = _s5_layer(us_c, us_l, {"wx": wx, "wy": wy, "p1": p1, "p2": p2})
    w0 = ffn_weights(0)
    w0.update({"s5_d": s5_d[0].reshape(1, S5_WIDTH), "w_glu": s5_w_glu[0].astype(BF16),
               "w_out": ev_w_out[0].astype(BF16)})
    x1, a, gt = _even_mid(x, mod_lat, att_l, ys_l, us_l, w0)
    x = _ffn_out(a, gt, x1, mod_lat, w0)
    c1, a, gt = _even_mid(ctx, mod_ctx, att_c, ys_c, us_c, w0)
    ctx = _ffn_out(a, gt, c1, mod_ctx, w0)

    mod_lat, mod_ctx = _layer_mod(mod, 1, b)
    sm = jax.nn.softmax(hg_lb, axis=0)
    lower = (jnp.cumsum(sm, axis=0) - sm[0])[1].reshape(2, 1, HG_WIDTH)
    w_hg = hg_w_in[0].astype(BF16)
    hg_c = _odd_in(ctx, mod_ctx, w_hg)
    hg_l = _odd_in(x, mod_lat, w_hg)
    zero_state = jnp.zeros((b, HG_HEADS, 2, HG_DV, HG_DK), F32)
    _, _, s_ctx = _hgrn_auto(hg_c, lower, zero_state)
    o_f, o_b, _ = _hgrn_auto(hg_l, lower, s_ctx)
    w1 = ffn_weights(1)
    w1.update({"hg_norm": hg_norm[0].reshape(1, HG_DV), "w_out": hg_w_out[0].astype(BF16)})
    x1, a, gt = _odd_mid(x, mod_lat, o_f, o_b, hg_l, w1)
    return _ffn_out(a, gt, x1, mod_lat, w1)
```

```python
import functools
import math

import jax
import jax.numpy as jnp
from jax import lax
from jax.experimental import pallas as pl
from jax.experimental.pallas import tpu as pltpu

F32 = jnp.float32
BF16 = jnp.bfloat16

D_MODEL = 1024
DEPTH = 2
GRID_W = 64
NORM_EPS = 1e-6
DN_ALPHA = (2.0 * DEPTH) ** 0.25

MLA_HEADS = 8
MLA_NOPE = 64
MLA_ROPE = 32
MLA_V = 64
MLA_Q_LORA = 384
MLA_KV_LORA = 256
MLA_SCALE = (MLA_NOPE + MLA_ROPE) ** -0.5
Q_SCALE = MLA_SCALE * math.log2(math.e)
ROPE_BASE = 10000.0

S5_WIDTH = 512
S5_GROUP = 16
S5_GROUPS = S5_WIDTH // S5_GROUP
S5_STATE = 64
S5_CHUNK = 16

HG_HEADS = 8
HG_DK = 128
HG_DV = 128
HG_WIDTH = HG_HEADS * HG_DK
HG_IN = 5 * HG_WIDTH
HG_CHUNK = 128

FFN_HIDDEN = 2816

LANE = 128
HEAD_PAD = 128
EVEN_IN_PAD = MLA_Q_LORA + MLA_KV_LORA + S5_WIDTH + 2 * LANE

VMEM_LIMIT = 56 * 1024 * 1024


def _params(*sem):
    return pltpu.CompilerParams(dimension_semantics=sem, vmem_limit_bytes=VMEM_LIMIT)


def _const_spec(shape):
    zeros = (0,) * len(shape)
    return pl.BlockSpec(shape, lambda *_: zeros, pipeline_mode=pl.Buffered(1))


def _dot(a, b):
    return jnp.dot(a, b, preferred_element_type=F32)


def _dot_nt(a, b):
    return lax.dot_general(a, b, (((1,), (1,)), ((), ())), preferred_element_type=F32)


def _dot_tn(a, b):
    return lax.dot_general(a, b, (((0,), (0,)), ((), ())), preferred_element_type=F32)


def _sigmoid(x):
    return 1.0 / (1.0 + jnp.exp(-x))


def _silu(x):
    return x * _sigmoid(x)


def _rms(x, g):
    return x * lax.rsqrt(jnp.mean(x * x, -1, keepdims=True) + NORM_EPS) * g


def _layer_norm(x, g, b):
    mu = jnp.mean(x, -1, keepdims=True)
    xc = x - mu
    var = jnp.mean(xc * xc, -1, keepdims=True)
    return xc * lax.rsqrt(var + NORM_EPS) * g + b


ADA_TN = 512
N_COND = 3


def _adaln_kernel(ct_ref, w_ref, b_ref, o_ref):
    s = _silu(ct_ref[...])
    w = w_ref[0]
    rows = [jnp.sum(s[:, r:r + 1] * w, axis=0, keepdims=True) for r in range(N_COND)]
    rows.append(jnp.zeros((8 - N_COND, w.shape[1]), F32))
    o_ref[0] = jnp.concatenate(rows, axis=0) + b_ref[0]


def _adaln(cond_t, ada_w, ada_b):
    depth, d, n = ada_w.shape
    return pl.pallas_call(
        _adaln_kernel,
        grid=(depth, n // ADA_TN),
        in_specs=[
            pl.BlockSpec((d, 8), lambda l, j: (0, 0)),
            pl.BlockSpec((1, d, ADA_TN), lambda l, j: (l, 0, j)),
            pl.BlockSpec((1, 1, ADA_TN), lambda l, j: (l, 0, j)),
        ],
        out_specs=pl.BlockSpec((1, 8, ADA_TN), lambda l, j: (l, 0, j)),
        out_shape=jax.ShapeDtypeStruct((depth, 8, n), F32),
        compiler_params=_params("arbitrary", "arbitrary"),
        name="adaln",
    )(cond_t, ada_w, ada_b.reshape(depth, 1, n))


def _even_in_kernel(x_ref, mod_ref, win_ref, qg_ref, kvg_ref, wq_ref, wqr_ref, wk_ref, wvt_ref,
                    cos_ref, sin_ref, q_ref, k_ref, vt_ref, us_ref):
    m = mod_ref[0]
    u = (x_ref[0] * (1.0 + m[1:2]) + m[0:1]).astype(BF16)
    h = _dot(u, win_ref[...])
    c0 = MLA_Q_LORA
    c1 = c0 + MLA_KV_LORA
    c2 = c1 + S5_WIDTH
    us_ref[0] = h[:, c1:c2].astype(BF16)
    cos = cos_ref[...]
    sin = sin_ref[...]
    kr = h[:, c2:c2 + LANE] * cos + h[:, c2 + LANE:c2 + 2 * LANE] * sin
    cqn = _rms(h[:, :c0], qg_ref[...]).astype(BF16)
    qa = _dot(cqn, wq_ref[...])
    qb = _dot(cqn, wqr_ref[...])
    kvn = _rms(h[:, c0:c1], kvg_ref[...]).astype(BF16)
    ka = _dot(kvn, wk_ref[...])
    vt = _dot_nt(wvt_ref[...], kvn).astype(BF16)
    ones = jnp.ones((V_ROWS - MLA_V, vt.shape[1]), BF16)
    for hh in range(MLA_HEADS):
        vt_ref[0, hh * V_ROWS:hh * V_ROWS + MLA_V] = vt[hh * MLA_V:(hh + 1) * MLA_V]
        vt_ref[0, hh * V_ROWS + MLA_V:(hh + 1) * V_ROWS] = ones
    for hh in range(MLA_HEADS):
        sl = slice(hh * HEAD_PAD, (hh + 1) * HEAD_PAD)
        q_ref[0, :, sl] = ((qa[:, sl] * cos + qb[:, sl] * sin) * Q_SCALE).astype(BF16)
        k_ref[0, :, sl] = (ka[:, sl] + kr).astype(BF16)


def _even_in(x, mod, w, cos, sin, tm):
    b, l, d = x.shape
    hp = MLA_HEADS * HEAD_PAD
    row = lambda n: pl.BlockSpec((1, tm, n), lambda bi, i: (bi, i, 0))
    return pl.pallas_call(
        _even_in_kernel,
        grid=(b, l // tm),
        in_specs=[
            row(d),
            pl.BlockSpec((1, 8, d), lambda bi, i: (bi, 0, 0)),
            _const_spec(w["w_in"].shape), _const_spec(w["q_norm"].shape), _const_spec(w["kv_norm"].shape),
            _const_spec(w["wq"].shape), _const_spec(w["wq_rot"].shape), _const_spec(w["wk"].shape),
            _const_spec(w["wv_t"].shape),
            pl.BlockSpec((tm, HEAD_PAD), lambda bi, i: (i, 0)),
            pl.BlockSpec((tm, HEAD_PAD), lambda bi, i: (i, 0)),
        ],
        out_specs=[row(hp), row(hp),
                   pl.BlockSpec((1, MLA_HEADS * V_ROWS, tm), lambda bi, i: (bi, 0, i)),
                   row(S5_WIDTH)],
        out_shape=[
            jax.ShapeDtypeStruct((b, l, hp), BF16),
            jax.ShapeDtypeStruct((b, l, hp), BF16),
            jax.ShapeDtypeStruct((b, MLA_HEADS * V_ROWS, l), BF16),
            jax.ShapeDtypeStruct((b, l, S5_WIDTH), BF16),
        ],
        compiler_params=_params("parallel", "parallel"),
        name="even_in",
    )(x, mod, w["w_in"], w["q_norm"], w["kv_norm"], w["wq"], w["wq_rot"], w["wk"], w["wv_t"], cos, sin)


ATT_TQ = 512
ATT_TK = 512
ATT_UNROLL = 4


HEADS_PER_STEP = 2
BF16_SUBLANES = 16
V_ROWS = MLA_V + BF16_SUBLANES


def _attn_kernel(q_ref, *refs, kv_lens):
    o_ref, s_buf = refs[-2:]
    kv_refs = refs[:-2]
    tq = q_ref.shape[1]
    tk = s_buf.shape[2]
    heads = range(HEADS_PER_STEP)
    q_t = [q_ref[0, :, hh * HEAD_PAD:(hh + 1) * HEAD_PAD].astype(F32).T.astype(BF16) for hh in heads]

    def fill(slot, src, start, size):
        kc = kv_refs[2 * src][0, pl.ds(start, size), :]
        tops = []
        for hh in heads:
            s = _dot(kc[:, hh * HEAD_PAD:(hh + 1) * HEAD_PAD], q_t[hh])
            s_buf[slot, hh, :size] = s
            tops.append(jnp.max(s, axis=0, keepdims=True))
        return tuple(tops)

    def drain(slot, src, start, size, tops, carry):
        vt = kv_refs[2 * src + 1][0, :, pl.ds(start, size)]
        new = []
        for hh in heads:
            m, acc = carry[hh]
            m_new = jnp.maximum(m, tops[hh])
            p = jnp.exp2(s_buf[slot, hh, :size] - m_new).astype(BF16)
            acc = jnp.exp2(m - m_new) * acc + _dot(vt[hh * V_ROWS:(hh + 1) * V_ROWS], p)
            new.append((m_new, acc))
        return tuple(new)

    carry = tuple((jnp.full((1, tq), -jnp.inf, F32), jnp.zeros((V_ROWS, tq), F32)) for _ in heads)
    chunks = [(src, j0, min(tk, n - j0)) for src, n in enumerate(kv_lens) for j0 in range(0, n, tk)]
    trips = (kv_lens[0] // tk - 1) // ATT_UNROLL if kv_lens[0] % tk == 0 else 0
    trips = trips if trips >= 2 else 0
    tops = fill(0, *chunks[0])
    if trips:
        main = lambda j: (0, pl.multiple_of(j * tk, tk), tk)

        def body(t, state):
            tops, c = state
            for i in range(ATT_UNROLL):
                j = t * ATT_UNROLL + i
                tops_next = fill((i + 1) % 2, *main(j + 1))
                c = drain(i % 2, *main(j), tops, c)
                tops = tops_next
            return tops, c

        tops, carry = lax.fori_loop(0, trips, body, (tops, carry))
    for j in range(trips * ATT_UNROLL, len(chunks)):
        tops_next = fill((j + 1) % 2, *chunks[j + 1]) if j + 1 < len(chunks) else None
        carry = drain(j % 2, *chunks[j], tops, carry)
        tops = tops_next
    out_t = jnp.concatenate([acc[:MLA_V] / acc[MLA_V:MLA_V + 1] for _, acc in carry], axis=0)
    o_ref[0] = out_t.T.astype(BF16)


def _attention(q, kvs):
    b, lq, _ = q.shape
    tq = min(ATT_TQ, lq)
    hs = HEADS_PER_STEP
    in_specs = [pl.BlockSpec((1, tq, hs * HEAD_PAD), lambda bi, hp, i: (bi, i, hp))]
    args = [q]
    for k, vt in kvs:
        n = k.shape[1]
        in_specs.append(pl.BlockSpec((1, n, hs * HEAD_PAD), lambda bi, hp, i: (bi, 0, hp)))
        in_specs.append(pl.BlockSpec((1, hs * V_ROWS, n), lambda bi, hp, i: (bi, hp, 0)))
        args += [k, vt]
    return pl.pallas_call(
        functools.partial(_attn_kernel, kv_lens=tuple(k.shape[1] for k, _ in kvs)),
        grid=(b, MLA_HEADS // hs, lq // tq),
        in_specs=in_specs,
        out_specs=pl.BlockSpec((1, tq, hs * MLA_V), lambda bi, hp, i: (bi, i, hp)),
        out_shape=jax.ShapeDtypeStruct((b, lq, MLA_HEADS * MLA_V), BF16),
        scratch_shapes=[pltpu.VMEM((2, hs, ATT_TK, tq), F32)],
        compiler_params=_params("parallel", "parallel", "parallel"),
        name="attention",
    )(*args)


def _s5_x_kernel(uc_ref, ul_ref, w_ref, xc_ref, xl_ref):
    xc_ref[...] = _dot(uc_ref[0], w_ref[0])
    xl_ref[...] = _dot(ul_ref[0], w_ref[0])


def _s5_chunk_inputs(u_ctx, u_lat, wx):
    g, _, n = u_lat.shape
    rows = lambda u: pl.BlockSpec((1, u.shape[1], n), lambda gi: (gi, 0, 0))
    cols = lambda u: pl.BlockSpec((u.shape[1], n), lambda gi: (0, gi))
    return pl.pallas_call(
        _s5_x_kernel,
        grid=(g,),
        in_specs=[rows(u_ctx), rows(u_lat), pl.BlockSpec((1, n, n), lambda gi: (gi, 0, 0))],
        out_specs=[cols(u_ctx), cols(u_lat)],
        out_shape=[jax.ShapeDtypeStruct((u.shape[1], g * n), F32) for u in (u_ctx, u_lat)],
        compiler_params=_params("parallel"),
        name="s5_chunk_inputs",
    )(u_ctx, u_lat, wx)


S5_SCAN_GROUPS = 16


def _s5_scan_kernel(xc_ref, xl_ref, lre_ref, lim_ref, sc_ref, sl_ref):
    ns = S5_STATE
    lre = lre_ref[...]
    lim = lim_ref[...]
    is_fwd = lax.broadcasted_iota(jnp.int32, lre.shape, 1) < ns

    def walk(x_ref, s_ref, carry):
        n = x_ref.shape[1]

        def step(i, carry):
            sre, sim = carry
            cf = i
            cb = n - 1 - i
            s_ref[0, cf, :, 0:ns] = sre[:, :ns]
            s_ref[0, cb, :, ns:2 * ns] = sre[:, ns:]
            s_ref[0, cf, :, 2 * ns:3 * ns] = sim[:, :ns]
            s_ref[0, cb, :, 3 * ns:] = sim[:, ns:]
            xf = x_ref[0, cf]
            xb = x_ref[0, cb]
            xre = jnp.where(is_fwd, xf[:, :2 * ns], xb[:, :2 * ns])
            xim = jnp.where(is_fwd, xf[:, 2 * ns:], xb[:, 2 * ns:])
            return sre * lre - sim * lim + xre, sre * lim + sim * lre + xim

        return lax.fori_loop(0, n, step, carry)

    zero = jnp.zeros(lre.shape, F32)
    walk(xl_ref, sl_ref, walk(xc_ref, sc_ref, (zero, zero)))


def _s5_scan(x_ctx, x_lat, p1, p2):
    b, _, g, w = x_lat.shape
    gb = S5_SCAN_GROUPS
    chunks = lambda x: pl.BlockSpec((1, x.shape[1], gb, w), lambda bi, gi: (bi, 0, gi, 0))
    pole = pl.BlockSpec((gb, w // 2), lambda bi, gi: (gi, 0))
    return pl.pallas_call(
        _s5_scan_kernel,
        grid=(b, g // gb),
        in_specs=[chunks(x_ctx), chunks(x_lat), pole, pole],
        out_specs=[chunks(x_ctx), chunks(x_lat)],
        out_shape=[jax.ShapeDtypeStruct(x.shape, F32) for x in (x_ctx, x_lat)],
        compiler_params=_params("parallel", "parallel"),
        name="s5_scan",
    )(x_ctx, x_lat, p1, p2)


def _s5_y_kernel(uc_ref, sc_ref, ul_ref, sl_ref, w_ref, yc_ref, yl_ref):
    n = ul_ref.shape[2]
    for u_ref, s_ref, y_ref in ((uc_ref, sc_ref, yc_ref), (ul_ref, sl_ref, yl_ref)):
        y = _dot(u_ref[0], w_ref[0, :n]) + _dot(s_ref[...].astype(BF16), w_ref[0, n:])
        y_ref[0] = y.astype(y_ref.dtype)


def _s5_outputs(u_ctx, s_ctx, u_lat, s_lat, wy):
    g, _, n = u_lat.shape
    rows = lambda u: pl.BlockSpec((1, u.shape[1], n), lambda gi: (gi, 0, 0))
    cols = lambda u: pl.BlockSpec((u.shape[1], n), lambda gi: (0, gi))
    return pl.pallas_call(
        _s5_y_kernel,
        grid=(g,),
        in_specs=[rows(u_ctx), cols(u_ctx), rows(u_lat), cols(u_lat),
                  pl.BlockSpec((1, 2 * n, n), lambda gi: (gi, 0, 0))],
        out_specs=[rows(u_ctx), rows(u_lat)],
        out_shape=[jax.ShapeDtypeStruct(u.shape, BF16) for u in (u_ctx, u_lat)],
        compiler_params=_params("parallel"),
        name="s5_outputs",
    )(u_ctx, s_ctx, u_lat, s_lat, wy)


def _cmul(ar, ai, br, bi):
    return ar * br - ai * bi, ar * bi + ai * br


def _s5_weights(lam_re, lam_im, log_dt, b_re, b_im, c_re, c_im):
    t = S5_CHUNK
    hi = lax.Precision.HIGHEST
    dt = jnp.exp(log_dt)[..., None]
    mag = jnp.exp(lam_re * dt)
    lb_re = mag * jnp.cos(lam_im * dt)
    lb_im = mag * jnp.sin(lam_im * dt)
    den = lam_re * lam_re + lam_im * lam_im
    nr = lb_re - 1.0
    fr = (nr * lam_re + lb_im * lam_im) / den
    fi = (lb_im * lam_re - nr * lam_im) / den
    bb_re = fr[..., None] * b_re - fi[..., None] * b_im
    bb_im = fr[..., None] * b_im + fi[..., None] * b_re
    pw = [(jnp.ones_like(lb_re), jnp.zeros_like(lb_im))]
    for _ in range(t):
        pw.append(_cmul(pw[-1][0], pw[-1][1], lb_re, lb_im))
    pw_re = jnp.stack([p[0] for p in pw])
    pw_im = jnp.stack([p[1] for p in pw])
    pb_re, pb_im = _cmul(pw_re[:t, ..., None], pw_im[:t, ..., None], bb_re[None], bb_im[None])
    kern = (jnp.einsum('dgpn,ldgnq->ldgpq', c_re, pb_re, precision=hi)
            - jnp.einsum('dgpn,ldgnq->ldgpq', c_im, pb_im, precision=hi))
    lag = jnp.arange(t)[None, :] - jnp.arange(t)[:, None]
    pick = lambda d: (d[..., None] == jnp.arange(t)).astype(F32)
    kf = jnp.einsum('stl,lgpq->stgpq', pick(lag), kern[:, 0], precision=hi)
    kb = jnp.einsum('stl,lgpq->stgpq', pick(-lag), kern[:, 1], precision=hi)
    g = lam_re.shape[1]
    n_lane = t * S5_GROUP
    toep = (kf + kb).transpose(2, 0, 4, 1, 3).reshape(g, n_lane, n_lane)

    def to_state(w):
        return w.transpose(1, 0, 3, 2).reshape(g, n_lane, S5_STATE)

    wx = jnp.concatenate([to_state(pb_re[::-1, 0]), to_state(pb_re[:, 1]),
                          to_state(pb_im[::-1, 0]), to_state(pb_im[:, 1])], axis=-1)

    def from_state(d, p_re, p_im):
        cr = c_re[d][None]
        ci = c_im[d][None]
        a = cr * p_re[:, :, None, :] - ci * p_im[:, :, None, :]
        bneg = -(cr * p_im[:, :, None, :] + ci * p_re[:, :, None, :])
        lanes = lambda w: w.transpose(1, 3, 0, 2).reshape(g, S5_STATE, n_lane)
        return lanes(a), lanes(bneg)

    f_re, f_im = from_state(0, pw_re[1:, 0], pw_im[1:, 0])
    b_re, b_im = from_state(1, pw_re[t:0:-1, 1], pw_im[t:0:-1, 1])
    wy = jnp.concatenate([toep, f_re, b_re, f_im, b_im], axis=1)
    lt_re, lt_im = pw_re[t], pw_im[t]
    p1 = jnp.concatenate([lt_re[0], lt_re[1]], axis=-1)
    p2 = jnp.concatenate([lt_im[0], lt_im[1]], axis=-1)
    return wx.astype(BF16), wy.astype(BF16), p1, p2


S5_RELAYOUT_CHUNKS = 64
LANE_GROUPS = LANE // S5_GROUP


def _to_groups_kernel(us_ref, u_ref, tok_ref):
    t, p = S5_CHUNK, S5_GROUP
    nc = u_ref.shape[1]
    tiles = S5_WIDTH // LANE
    for q in range(tiles):
        tok_ref[q] = us_ref[0, :, q * LANE:(q + 1) * LANE].astype(F32)
    for s in range(t):
        si = s % LANE_GROUPS
        for q in range(tiles):
            piece = tok_ref[q, pl.ds(s, nc, stride=t), :]
            for gi in range(LANE_GROUPS):
                shift = ((si - gi) * p) % LANE
                moved = pltpu.roll(piece, shift, 1) if shift else piece
                u_ref[q * LANE_GROUPS + gi, :, s * p:(s + 1) * p] = moved[:, si * p:(si + 1) * p].astype(BF16)


def _from_groups_kernel(y_ref, o_ref, step_ref, tok_ref):
    t, p = S5_CHUNK, S5_GROUP
    nc = y_ref.shape[1]
    for g in range(S5_GROUPS):
        q, gi = divmod(g, LANE_GROUPS)
        for so in range(t * p // LANE):
            piece = y_ref[g, :, so * LANE:(so + 1) * LANE].astype(F32)
            for si in range(LANE_GROUPS):
                shift = ((gi - si) * p) % LANE
                moved = pltpu.roll(piece, shift, 1) if shift else piece
                step_ref[so * LANE_GROUPS + si, q, :, gi * p:(gi + 1) * p] = moved[:, gi * p:(gi + 1) * p]
    for q in range(S5_WIDTH // LANE):
        for s in range(t):
            tok_ref[q, pl.ds(s, nc, stride=t), :] = step_ref[s, q]
        o_ref[0, :, q * LANE:(q + 1) * LANE] = tok_ref[q].astype(o_ref.dtype)


def _relayout_tile(n_chunks):
    return min(S5_RELAYOUT_CHUNKS, n_chunks)


def _s5_to_groups(us):
    b, n, w = us.shape
    t = S5_CHUNK
    nc = _relayout_tile(n // t)
    steps = n // t // nc
    return pl.pallas_call(
        _to_groups_kernel,
        grid=(b, steps),
        in_specs=[pl.BlockSpec((1, nc * t, w), lambda bi, i: (bi, i, 0))],
        out_specs=pl.BlockSpec((S5_GROUPS, nc, t * S5_GROUP), lambda bi, i: (0, bi * steps + i, 0)),
        out_shape=jax.ShapeDtypeStruct((S5_GROUPS, b * (n // t), t * S5_GROUP), BF16),
        scratch_shapes=[pltpu.VMEM((w // LANE, nc * t, LANE), F32)],
        compiler_params=_params("parallel", "parallel"),
        name="s5_to_groups",
    )(us)


def _s5_from_groups(y, b):
    g, r, tp = y.shape
    t = S5_CHUNK
    n = r // b * t
    nc = _relayout_tile(n // t)
    steps = n // t // nc
    w = g * S5_GROUP
    return pl.pallas_call(
        _from_groups_kernel,
        grid=(b, steps),
        in_specs=[pl.BlockSpec((g, nc, tp), lambda bi, i: (0, bi * steps + i, 0))],
        out_specs=pl.BlockSpec((1, nc * t, w), lambda bi, i: (bi, i, 0)),
        out_shape=jax.ShapeDtypeStruct((b, n, w), BF16),
        scratch_shapes=[pltpu.VMEM((t, w // LANE, nc, LANE), F32), pltpu.VMEM((w // LANE, nc * t, LANE), F32)],
        compiler_params=_params("parallel", "parallel"),
        name="s5_from_groups",
    )(y)


def _s5_layer(us_ctx, us_lat, w):
    b = us_ctx.shape[0]
    g = S5_GROUPS
    u_c, u_l = _s5_to_groups(us_ctx), _s5_to_groups(us_lat)
    x_c, x_l = _s5_chunk_inputs(u_c, u_l, w["wx"])
    as_chunks = lambda x: x.reshape(b, x.shape[0] // b, g, 2 * LANE)
    s_c, s_l = _s5_scan(as_chunks(x_c), as_chunks(x_l), w["p1"], w["p2"])
    y_c, y_l = _s5_outputs(u_c, s_c.reshape(x_c.shape), u_l, s_l.reshape(x_l.shape), w["wy"])
    return _s5_from_groups(y_c, b), _s5_from_groups(y_l, b)


FFN_CHUNK = FFN_HIDDEN // 2


def _mid_tail(x, y, m, wffn_ref, lng_ref, lnb_ref, x1_ref, a_ref, gt_ref):
    x1 = _layer_norm(DN_ALPHA * x + m[2:3] * y, lng_ref[...], lnb_ref[...])
    x1_ref[0] = x1
    uf = (x1 * (1.0 + m[4:5]) + m[3:4]).astype(BF16)
    for j in range(FFN_HIDDEN // FFN_CHUNK):
        sl = slice(j * FFN_CHUNK, (j + 1) * FFN_CHUNK)
        a_ref[0, :, sl] = _dot(uf, wffn_ref[:, sl]).astype(BF16)
        gsl = slice(FFN_HIDDEN + j * FFN_CHUNK, FFN_HIDDEN + (j + 1) * FFN_CHUNK)
        gt_ref[0, :, sl] = _dot(uf, wffn_ref[:, gsl]).astype(BF16)


def _gelu_tanh(x):
    return 0.5 * x * (1.0 + jnp.tanh(math.sqrt(2.0 / math.pi) * (x + 0.044715 * x * x * x)))


def _even_mid_kernel(x_ref, mod_ref, att_ref, ys_ref, us_ref, d_ref, wglu_ref, wout_ref, lng_ref, lnb_ref,
                     wffn_ref, x1_ref, a_ref, gt_ref):
    z = _gelu_tanh(ys_ref[0].astype(F32) + d_ref[...] * us_ref[0].astype(F32))
    s5 = z * _sigmoid(_dot(z.astype(BF16), wglu_ref[...]))
    n_att = att_ref.shape[2]
    y = _dot(att_ref[0], wout_ref[:n_att]) + _dot(s5.astype(BF16), wout_ref[n_att:])
    _mid_tail(x_ref[0], y, mod_ref[0], wffn_ref, lng_ref, lnb_ref, x1_ref, a_ref, gt_ref)


def _odd_mid_kernel(x_ref, mod_ref, of_ref, ob_ref, g_ref, hn_ref, wout_ref, lng_ref, lnb_ref,
                    wffn_ref, x1_ref, a_ref, gt_ref):
    o = of_ref[0] + ob_ref[0]
    g = g_ref[0]
    gate = g * _sigmoid(g)
    parts = []
    for hh in range(HG_HEADS):
        sl = slice(hh * HG_DV, (hh + 1) * HG_DV)
        parts.append((_rms(o[:, sl], hn_ref[...]) * gate[:, sl]).astype(BF16))
    y = _dot(jnp.concatenate(parts, axis=-1), wout_ref[...])
    _mid_tail(x_ref[0], y, mod_ref[0], wffn_ref, lng_ref, lnb_ref, x1_ref, a_ref, gt_ref)


MID_TM = 512


def _mid_call(kernel, name, x, mod, row_args, row_specs, const_args):
    b, l, d = x.shape
    tm = min(MID_TM, l)
    row = lambda n: pl.BlockSpec((1, tm, n), lambda bi, i: (bi, i, 0))
    in_specs = [row(d), pl.BlockSpec((1, 8, d), lambda bi, i: (bi, 0, 0))]
    in_specs += [spec(tm) for spec in row_specs]
    in_specs += [_const_spec(a.shape) for a in const_args]
    return pl.pallas_call(
        kernel,
        grid=(b, l // tm),
        in_specs=in_specs,
        out_specs=[row(d), row(FFN_HIDDEN), row(FFN_HIDDEN)],
        out_shape=[jax.ShapeDtypeStruct((b, l, d), F32),
                   jax.ShapeDtypeStruct((b, l, FFN_HIDDEN), BF16),
                   jax.ShapeDtypeStruct((b, l, FFN_HIDDEN), BF16)],
        compiler_params=_params("parallel", "parallel"),
        name=name,
    )(x, mod, *row_args, *const_args)


def _row_spec(n, lane_block=0):
    return lambda tm: pl.BlockSpec((1, tm, n), lambda bi, i: (bi, i, lane_block))


def _even_mid(x, mod, att, ys, us, w):
    return _mid_call(
        _even_mid_kernel, "even_mid", x, mod, [att, ys, us],
        [_row_spec(att.shape[2]), _row_spec(S5_WIDTH), _row_spec(S5_WIDTH)],
        [w["s5_d"], w["w_glu"], w["w_out"], w["ln_g0"], w["ln_b0"], w["ffn_w_in"]])


def _odd_mid(x, mod, o_f, o_b, hg, w):
    return _mid_call(
        _odd_mid_kernel, "odd_mid", x, mod, [o_f, o_b, hg],
        [_row_spec(HG_WIDTH), _row_spec(HG_WIDTH), _row_spec(HG_WIDTH, 4)],
        [w["hg_norm"], w["w_out"], w["ln_g0"], w["ln_b0"], w["ffn_w_in"]])


FFN_TM = 256
HALO = 8


def _ffn_out_kernel(a_ref, ap_ref, an_ref, gt_ref, x1_ref, mod_ref, cw_ref, cb_ref, wout_ref, lng_ref, lnb_ref,
                    o_ref):
    i = pl.program_id(1)
    tm = a_ref.shape[1]
    a = a_ref[0].astype(F32)
    row = lax.broadcasted_iota(jnp.int32, (tm, 1), 0)
    before = jnp.where(i == 0, 0.0, ap_ref[0, HALO - 1:HALO, :].astype(F32))
    after = jnp.where(i == pl.num_programs(1) - 1, 0.0, an_ref[0, 0:1, :].astype(F32))
    a_prev = jnp.where(row == 0, before, pltpu.roll(a, 1, 0))
    a_next = jnp.where(row == tm - 1, after, pltpu.roll(a, tm - 1, 0))
    cw = cw_ref[...]
    conv = cb_ref[...] + cw[0:1] * a_prev + cw[1:2] * a + cw[2:3] * a_next
    hidden = (_silu(conv) * gt_ref[0].astype(F32)).astype(BF16)
    f = _dot(hidden, wout_ref[...])
    m = mod_ref[0]
    o_ref[0] = _layer_norm(DN_ALPHA * x1_ref[0] + m[5:6] * f, lng_ref[...], lnb_ref[...])


def _ffn_out(a, gt, x1, mod, w):
    b, l, d = x1.shape
    tm = min(FFN_TM, l)
    per = tm // HALO
    last = l // HALO - 1
    row = lambda n: pl.BlockSpec((1, tm, n), lambda bi, i: (bi, i, 0))
    consts = [w["conv_w"], w["conv_b"], w["ffn_w_out"], w["ln_g1"], w["ln_b1"]]
    return pl.pallas_call(
        _ffn_out_kernel,
        grid=(b, l // tm),
        in_specs=[
            row(FFN_HIDDEN),
            pl.BlockSpec((1, HALO, FFN_HIDDEN), lambda bi, i: (bi, jnp.maximum(i * per - 1, 0), 0)),
            pl.BlockSpec((1, HALO, FFN_HIDDEN), lambda bi, i: (bi, jnp.minimum((i + 1) * per, last), 0)),
            row(FFN_HIDDEN), row(d),
            pl.BlockSpec((1, 8, d), lambda bi, i: (bi, 0, 0)),
        ] + [_const_spec(c.shape) for c in consts],
        out_specs=row(d),
        out_shape=jax.ShapeDtypeStruct((b, l, d), F32),
        compiler_params=_params("parallel", "parallel"),
        name="ffn_out",
    )(a, a, a, gt, x1, mod, *consts)


ODD_TM = 256


def _odd_in_kernel(x_ref, mod_ref, w_ref, o_ref):
    m = mod_ref[0]
    u = (x_ref[0] * (1.0 + m[1:2]) + m[0:1]).astype(BF16)
    for j in range(HG_IN // HG_WIDTH):
        sl = slice(j * HG_WIDTH, (j + 1) * HG_WIDTH)
        o_ref[0, :, sl] = _dot(u, w_ref[:, sl])


def _odd_in(x, mod, w_in):
    b, l, d = x.shape
    tm = min(ODD_TM, l)
    return pl.pallas_call(
        _odd_in_kernel,
        grid=(b, l // tm),
        in_specs=[pl.BlockSpec((1, tm, d), lambda bi, i: (bi, i, 0)),
                  pl.BlockSpec((1, 8, d), lambda bi, i: (bi, 0, 0)),
                  _const_spec(w_in.shape)],
        out_specs=pl.BlockSpec((1, tm, HG_IN), lambda bi, i: (bi, i, 0)),
        out_shape=jax.ShapeDtypeStruct((b, l, HG_IN), F32),
        compiler_params=_params("parallel", "parallel"),
        name="odd_in",
    )(x, mod, w_in)


def _hgrn_direction(q, pre, v, lb, state, level, reverse):
    c = q.shape[0]
    row = lax.broadcasted_iota(jnp.int32, q.shape, 0)
    pos = (c - 1 - row) if reverse else row

    def prev(x, k):
        return pltpu.roll(x, (c - k) if reverse else k, 0)

    def nxt(x, k):
        return pltpu.roll(x, k if reverse else (c - k), 0)

    f = lb + (1.0 - lb) * _sigmoid(pre)
    key = 1.0 - f
    cum = jnp.log(f)
    k = 1
    while k < c:
        cum = cum + jnp.where(pos >= k, prev(cum, k), 0.0)
        k *= 2
    last = cum[0:1] if reverse else cum[c - 1:c]

    qb = q.astype(BF16)
    kb = key.astype(BF16)
    scores = jnp.where(level == 0, _dot_nt(qb, kb), 0.0)
    z = cum
    half = 1
    lvl = 1
    while half < c:
        ref = jnp.where((pos & half) == 0, z, prev(z, half))
        e = jnp.exp(-jnp.abs(cum - ref))
        x = (jnp.where((pos & half) != 0, q, key) * e).astype(BF16)
        scores = jnp.where(level == lvl, _dot_nt(x, x), scores)
        z = jnp.where((pos & half) != 0, z, nxt(z, half))
        half *= 2
        lvl += 1

    o = _dot(scores.astype(BF16), v)
    o = o + _dot_nt((q * jnp.exp(cum)).astype(BF16), state.astype(BF16))
    kd = (key * jnp.exp(last - cum)).astype(BF16)
    new_state = jnp.exp(last) * state + _dot_tn(v, kd)
    return o, new_state


def _hgrn_kernel(qf_ref, pf_ref, vf_ref, qb_ref, pb_ref, vb_ref, lb_ref, lvf_ref, lvb_ref, s0_ref,
                 of_ref, ob_ref, sfin_ref, st_ref):
    ci = pl.program_id(2)

    @pl.when(ci == 0)
    def _():
        st_ref[...] = s0_ref[0, 0]

    o, s = _hgrn_direction(qf_ref[0], pf_ref[0], vf_ref[0].astype(BF16), lb_ref[0], st_ref[0],
                           lvf_ref[...], False)
    of_ref[0] = o
    st_ref[0] = s
    o, s = _hgrn_direction(qb_ref[0], pb_ref[0], vb_ref[0].astype(BF16), lb_ref[1], st_ref[1],
                           lvb_ref[...], True)
    ob_ref[0] = o
    st_ref[1] = s

    @pl.when(ci == pl.num_programs(2) - 1)
    def _():
        sfin_ref[0, 0] = st_ref[...]


def _hgrn_levels(c):
    p = jnp.arange(c, dtype=jnp.int32)
    x = p[:, None] ^ p[None, :]
    lvl = jnp.where(x == 0, 0, 32 - lax.clz(x))
    fwd = jnp.where(p[:, None] >= p[None, :], lvl, -1)
    bwd = jnp.where(p[:, None] <= p[None, :], lvl, -1)
    return fwd.astype(jnp.int32), bwd.astype(jnp.int32)


def _hgrn(hg, lb, s0):
    b, l, _ = hg.shape
    c = HG_CHUNK
    n = l // c
    nh = HG_HEADS
    lvf, lvb = _hgrn_levels(c)
    fw = lambda blk: pl.BlockSpec((1, c, HG_DK), lambda bi, h, ci: (bi, ci, blk * nh + h))
    bw = lambda blk: pl.BlockSpec((1, c, HG_DK), lambda bi, h, ci: (bi, n - 1 - ci, blk * nh + h))
    state_spec = pl.BlockSpec((1, 1, 2, HG_DV, HG_DK), lambda bi, h, ci: (bi, h, 0, 0, 0))
    return pl.pallas_call(
        _hgrn_kernel,
        grid=(b, nh, n),
        in_specs=[fw(0), fw(1), fw(3), bw(0), bw(2), bw(3),
                  pl.BlockSpec((2, 1, HG_DK), lambda bi, h, ci: (0, 0, h)),
                  pl.BlockSpec((c, c), lambda bi, h, ci: (0, 0)),
                  pl.BlockSpec((c, c), lambda bi, h, ci: (0, 0)),
                  state_spec],
        out_specs=[pl.BlockSpec((1, c, HG_DV), lambda bi, h, ci: (bi, ci, h)),
                   pl.BlockSpec((1, c, HG_DV), lambda bi, h, ci: (bi, n - 1 - ci, h)),
                   state_spec],
        out_shape=[jax.ShapeDtypeStruct((b, l, HG_WIDTH), F32),
                   jax.ShapeDtypeStruct((b, l, HG_WIDTH), F32),
                   jax.ShapeDtypeStruct((b, nh, 2, HG_DV, HG_DK), F32)],
        scratch_shapes=[pltpu.VMEM((2, HG_DV, HG_DK), F32)],
        compiler_params=_params("parallel", "parallel", "arbitrary"),
        name="hgrn",
    )(hg, hg, hg, hg, hg, hg, lb, lvf, lvb, s0)


HG_DIRECT_CHUNK = 64
HG_DIRECT_BLOCK = 512
HG_DIRECT_MAX_EXPONENT = 60.0
HG_DIRECT_MIN_LB = math.exp(-HG_DIRECT_MAX_EXPONENT / (HG_DIRECT_CHUNK // 2))


def _split3(x):
    hi = x.astype(BF16)
    r = x - hi.astype(F32)
    mid = r.astype(BF16)
    lo = (r - mid.astype(F32)).astype(BF16)
    return hi, mid, lo


def _hgrn_direct_block(q_ref, p_ref, v_ref, o_ref, lb, state, tri3, visible, reverse):
    c = HG_DIRECT_CHUNK
    n_sub = q_ref.shape[1] // c
    half = c // 2
    order = list(range(n_sub - 1, -1, -1) if reverse else range(n_sub))
    rows = [slice(j * c, (j + 1) * c) for j in order]
    fs = [lb + (1.0 - lb) * _sigmoid(p_ref[0, r, :]) for r in rows]
    cums = [_dot(tri3, jnp.concatenate(_split3(jnp.log(f)), axis=0)) for f in fs]
    mids = [cum[half:half + 1] if reverse else cum[half - 1:half] for cum in cums]
    lasts = [cum[0:1] if reverse else cum[c - 1:c] for cum in cums]
    qts = [(q_ref[0, r, :] * jnp.exp(cum - mid)).astype(BF16) for r, cum, mid in zip(rows, cums, mids)]
    kts = [((1.0 - f) * jnp.exp(mid - cum)).astype(BF16) for f, cum, mid in zip(fs, cums, mids)]
    vs = [v_ref[0, r, :].astype(BF16) for r in rows]
    scores = [jnp.where(visible, _dot_nt(qt, kt), 0.0).astype(BF16) for qt, kt in zip(qts, kts)]
    adds = [jnp.exp(last - mid) * _dot_tn(v, kt) for last, mid, v, kt in zip(lasts, mids, vs, kts)]
    intra = [_dot(sc, v) for sc, v in zip(scores, vs)]
    entering = []
    for mid, last, add in zip(mids, lasts, adds):
        entering.append((state * jnp.exp(mid)).astype(BF16))
        state = jnp.exp(last) * state + add
    for r, qt, o, s_in in zip(rows, qts, intra, entering):
        o_ref[0, r, :] = o + _dot_nt(qt, s_in)
    return state


def _hgrn_direct_kernel(qf_ref, pf_ref, vf_ref, qb_ref, pb_ref, vb_ref, lb_ref, trif_ref, trib_ref, s0_ref,
                        of_ref, ob_ref, sfin_ref, st_ref):
    ci = pl.program_id(2)
    c = HG_DIRECT_CHUNK

    @pl.when(ci == 0)
    def _():
        st_ref[...] = s0_ref[0, 0]

    row = lax.broadcasted_iota(jnp.int32, (c, c), 0)
    col = lax.broadcasted_iota(jnp.int32, (c, c), 1)
    st_ref[0] = _hgrn_direct_block(qf_ref, pf_ref, vf_ref, of_ref, lb_ref[0], st_ref[0], trif_ref[...],
                                   row >= col, False)
    st_ref[1] = _hgrn_direct_block(qb_ref, pb_ref, vb_ref, ob_ref, lb_ref[1], st_ref[1], trib_ref[...],
                                   row <= col, True)

    @pl.when(ci == pl.num_programs(2) - 1)
    def _():
        sfin_ref[0, 0] = st_ref[...]


def _hgrn_direct(hg, lb, s0):
    b, l, _ = hg.shape
    c = HG_DIRECT_CHUNK
    blk_rows = min(HG_DIRECT_BLOCK, l)
    n = l // blk_rows
    nh = HG_HEADS
    p = jnp.arange(c)
    lower_tri = (p[:, None] >= p[None, :]).astype(BF16)
    trif = jnp.concatenate([lower_tri] * 3, axis=1)
    trib = jnp.concatenate([lower_tri.T] * 3, axis=1)
    fw = lambda blk: pl.BlockSpec((1, blk_rows, HG_DK), lambda bi, h, ci: (bi, ci, blk * nh + h))
    bw = lambda blk: pl.BlockSpec((1, blk_rows, HG_DK), lambda bi, h, ci: (bi, n - 1 - ci, blk * nh + h))
    state_spec = pl.BlockSpec((1, 1, 2, HG_DV, HG_DK), lambda bi, h, ci: (bi, h, 0, 0, 0))
    return pl.pallas_call(
        _hgrn_direct_kernel,
        grid=(b, nh, n),
        in_specs=[fw(0), fw(1), fw(3), bw(0), bw(2), bw(3),
                  pl.BlockSpec((2, 1, HG_DK), lambda bi, h, ci: (0, 0, h)),
                  pl.BlockSpec((c, 3 * c), lambda bi, h, ci: (0, 0)),
                  pl.BlockSpec((c, 3 * c), lambda bi, h, ci: (0, 0)),
                  state_spec],
        out_specs=[pl.BlockSpec((1, blk_rows, HG_DV), lambda bi, h, ci: (bi, ci, h)),
                   pl.BlockSpec((1, blk_rows, HG_DV), lambda bi, h, ci: (bi, n - 1 - ci, h)),
                   state_spec],
        out_shape=[jax.ShapeDtypeStruct((b, l, HG_WIDTH), F32),
                   jax.ShapeDtypeStruct((b, l, HG_WIDTH), F32),
                   jax.ShapeDtypeStruct((b, nh, 2, HG_DV, HG_DK), F32)],
        scratch_shapes=[pltpu.VMEM((2, HG_DV, HG_DK), F32)],
        compiler_params=_params("parallel", "parallel", "arbitrary"),
        name="hgrn_direct",
    )(hg, hg, hg, hg, hg, hg, lb, trif, trib, s0)


def _hgrn_auto(hg, lb, s0):
    return lax.cond(jnp.min(lb) >= HG_DIRECT_MIN_LB,
                    lambda: tuple(_hgrn_direct(hg, lb, s0)), lambda: tuple(_hgrn(hg, lb, s0)))


def _rope_tables(length):
    rows = length // GRID_W
    row = jnp.repeat(jnp.arange(rows, dtype=F32), GRID_W)
    col = jnp.tile(jnp.arange(GRID_W, dtype=F32), rows)
    n_freq = MLA_ROPE // 4
    inv = ROPE_BASE ** (-jnp.arange(n_freq, dtype=F32) / n_freq)
    ar = row[:, None] * inv
    ac = col[:, None] * inv
    ang = jnp.concatenate([ar, ar, ac, ac], axis=-1)
    return jnp.cos(ang), jnp.sin(ang)


def _head_tables(cos, sin):
    n = cos.shape[0]
    pad = jnp.zeros((n, HEAD_PAD - MLA_NOPE - MLA_ROPE), F32)
    return (jnp.concatenate([jnp.ones((n, MLA_NOPE), F32), cos, pad], -1),
            jnp.concatenate([jnp.zeros((n, MLA_NOPE), F32), sin, pad], -1))


def _rotate_cols(w):
    ws = w.reshape(w.shape[:-1] + (2, 2, MLA_ROPE // 4))
    return jnp.stack([-ws[..., 1, :], ws[..., 0, :]], axis=-2).reshape(w.shape)


def _even_weights(ev_w_in, q_norm, w_uq, kv_norm, w_ukv):
    c0 = MLA_Q_LORA
    c1 = c0 + MLA_KV_LORA
    c2 = c1 + MLA_ROPE
    d = ev_w_in.shape[0]
    w_kr = ev_w_in[:, c1:c2]
    lo = jnp.zeros((d, MLA_NOPE), F32)
    hi = jnp.zeros((d, HEAD_PAD - MLA_NOPE - MLA_ROPE), F32)
    w_in = jnp.concatenate([ev_w_in[:, :c1], ev_w_in[:, c2:], lo, w_kr, hi, lo, _rotate_cols(w_kr), hi], axis=-1)
    wq = w_uq.reshape(c0, MLA_HEADS, MLA_NOPE + MLA_ROPE)
    zq = jnp.zeros((c0, MLA_HEADS, HEAD_PAD - MLA_NOPE - MLA_ROPE), F32)
    wq_pad = jnp.concatenate([wq, zq], -1).reshape(c0, MLA_HEADS * HEAD_PAD)
    wq_rot = jnp.concatenate([jnp.zeros((c0, MLA_HEADS, MLA_NOPE), F32), _rotate_cols(wq[..., MLA_NOPE:]), zq],
                             -1).reshape(c0, MLA_HEADS * HEAD_PAD)
    wkv = w_ukv.reshape(MLA_KV_LORA, MLA_HEADS, MLA_NOPE + MLA_V)
    wk = jnp.concatenate([wkv[..., :MLA_NOPE], jnp.zeros((MLA_KV_LORA, MLA_HEADS, HEAD_PAD - MLA_NOPE), F32)],
                         -1).reshape(MLA_KV_LORA, MLA_HEADS * HEAD_PAD)
    wv_t = wkv[..., MLA_NOPE:].reshape(MLA_KV_LORA, MLA_HEADS * MLA_V).T
    return {"w_in": w_in.astype(BF16), "q_norm": q_norm.reshape(1, -1), "kv_norm": kv_norm.reshape(1, -1),
            "wq": wq_pad.astype(BF16), "wq_rot": wq_rot.astype(BF16), "wk": wk.astype(BF16),
            "wv_t": wv_t.astype(BF16)}


def _layer_mod(mod, layer, b):
    d = D_MODEL
    rows = mod[layer, :, :6 * d].reshape(8, 6, d)
    pad = jnp.zeros((2, d), F32)
    lat = jnp.stack([jnp.concatenate([rows[bi], pad], 0) for bi in range(b)])
    ctx = jnp.broadcast_to(jnp.concatenate([rows[b], pad], 0)[None], (b, 8, d))
    return lat, ctx


def kernel(x, c, ctx, c_ctx, ada_w, ada_b, ln_g, ln_b, ffn_w_in, ffn_conv_w, ffn_conv_b, ffn_w_out, ev_w_in,
           mla_q_norm, mla_w_uq, mla_kv_norm, mla_w_ukv, s5_lam_re, s5_lam_im, s5_log_dt, s5_b_re, s5_b_im,
           s5_c_re, s5_c_im, s5_d, s5_w_glu, ev_w_out, hg_w_in, hg_lb, hg_norm, hg_w_out):
    b, l, d = x.shape
    lc = ctx.shape[1]
    assert b + 1 == N_COND and d == D_MODEL and l % 512 == 0 and lc % 256 == 0

    cond_t = jnp.concatenate([c, c_ctx[None], jnp.zeros((8 - N_COND, d), F32)], 0).T
    mod = _adaln(cond_t, ada_w, ada_b)

    def ffn_weights(layer):
        return {"ln_g0": ln_g[layer, 0].reshape(1, d), "ln_b0": ln_b[layer, 0].reshape(1, d),
                "ln_g1": ln_g[layer, 1].reshape(1, d), "ln_b1": ln_b[layer, 1].reshape(1, d),
                "ffn_w_in": ffn_w_in[layer].astype(BF16),
                "conv_w": jnp.concatenate([ffn_conv_w[layer], jnp.zeros((5, FFN_HIDDEN), F32)], 0),
                "conv_b": ffn_conv_b[layer].reshape(1, FFN_HIDDEN),
                "ffn_w_out": ffn_w_out[layer].astype(BF16)}

    mod_lat, mod_ctx = _layer_mod(mod, 0, b)
    we = _even_weights(ev_w_in[0], mla_q_norm[0], mla_w_uq[0], mla_kv_norm[0], mla_w_ukv[0])
    cos_l, sin_l = _head_tables(*_rope_tables(l))
    cos_c, sin_c = _head_tables(jnp.ones((lc, MLA_ROPE), F32), jnp.zeros((lc, MLA_ROPE), F32))
    q_l, k_l, v_l, us_l = _even_in(x, mod_lat, we, cos_l, sin_l, min(512, l))
    q_c, k_c, v_c, us_c = _even_in(ctx, mod_ctx, we, cos_c, sin_c, min(512, lc))
    att_l = _attention(q_l, [(k_l, v_l), (k_c, v_c)])
    att_c = _attention(q_c, [(k_c, v_c)])
    wx, wy, p1, p2 = _s5_weights(s5_lam_re[0], s5_lam_im[0], s5_log_dt[0], s5_b_re[0], s5_b_im[0],
                                 s5_c_re[0], s5_c_im[0])
    ys_c, ys_l = _s5_layer(us_c, us_l, {"wx": wx, "wy": wy, "p1": p1, "p2": p2})
    w0 = ffn_weights(0)
    w0.update({"s5_d": s5_d[0].reshape(1, S5_WIDTH), "w_glu": s5_w_glu[0].astype(BF16),
               "w_out": ev_w_out[0].astype(BF16)})
    x1, a, gt = _even_mid(x, mod_lat, att_l, ys_l, us_l, w0)
    x = _ffn_out(a, gt, x1, mod_lat, w0)
    c1, a, gt = _even_mid(ctx, mod_ctx, att_c, ys_c, us_c, w0)
    ctx = _ffn_out(a, gt, c1, mod_ctx, w0)

    mod_lat, mod_ctx = _layer_mod(mod, 1, b)
    sm = jax.nn.softmax(hg_lb, axis=0)
    lower = (jnp.cumsum(sm, axis=0) - sm[0])[1].reshape(2, 1, HG_WIDTH)
    w_hg = hg_w_in[0].astype(BF16)
    hg_c = _odd_in(ctx, mod_ctx, w_hg)
    hg_l = _odd_in(x, mod_lat, w_hg)
    zero_state = jnp.zeros((b, HG_HEADS, 2, HG_DV, HG_DK), F32)
    _, _, s_ctx = _hgrn_auto(hg_c, lower, zero_state)
    o_f, o_b, _ = _hgrn_auto(hg_l, lower, s_ctx)
    w1 = ffn_weights(1)
    w1.update({"hg_norm": hg_norm[0].reshape(1, HG_DV), "w_out": hg_w_out[0].astype(BF16)})
    x1, a, gt = _odd_mid(x, mod_lat, o_f, o_b, hg_l, w1)
    return _ffn_out(a, gt, x1, mod_lat, w1)
```

```python
import functools
import math

import jax
import jax.numpy as jnp
from jax import lax
from jax.experimental import pallas as pl
from jax.experimental.pallas import tpu as pltpu

F32 = jnp.float32
BF16 = jnp.bfloat16

D_MODEL = 1024
DEPTH = 2
GRID_W = 64
NORM_EPS = 1e-6
DN_ALPHA = (2.0 * DEPTH) ** 0.25

MLA_HEADS = 8
MLA_NOPE = 64
MLA_ROPE = 32
MLA_V = 64
MLA_Q_LORA = 384
MLA_KV_LORA = 256
MLA_SCALE = (MLA_NOPE + MLA_ROPE) ** -0.5
Q_SCALE = MLA_SCALE * math.log2(math.e)
ROPE_BASE = 10000.0

S5_WIDTH = 512
S5_GROUP = 16
S5_GROUPS = S5_WIDTH // S5_GROUP
S5_STATE = 64
S5_CHUNK = 16

HG_HEADS = 8
HG_DK = 128
HG_DV = 128
HG_WIDTH = HG_HEADS * HG_DK
HG_IN = 5 * HG_WIDTH
HG_CHUNK = 128

FFN_HIDDEN = 2816

LANE = 128
HEAD_PAD = 128
EVEN_IN_PAD = MLA_Q_LORA + MLA_KV_LORA + S5_WIDTH + 2 * LANE

VMEM_LIMIT = 56 * 1024 * 1024


def _params(*sem):
    return pltpu.CompilerParams(dimension_semantics=sem, vmem_limit_bytes=VMEM_LIMIT)


def _const_spec(shape):
    zeros = (0,) * len(shape)
    return pl.BlockSpec(shape, lambda *_: zeros, pipeline_mode=pl.Buffered(1))


def _dot(a, b):
    return jnp.dot(a, b, preferred_element_type=F32)


def _dot_nt(a, b):
    return lax.dot_general(a, b, (((1,), (1,)), ((), ())), preferred_element_type=F32)


def _dot_tn(a, b):
    return lax.dot_general(a, b, (((0,), (0,)), ((), ())), preferred_element_type=F32)


def _sigmoid(x):
    return 1.0 / (1.0 + jnp.exp2(x * -math.log2(math.e)))


def _silu(x):
    return x * _sigmoid(x)


def _rms(x, g):
    return x * lax.rsqrt(jnp.mean(x * x, -1, keepdims=True) + NORM_EPS) * g


def _layer_norm(x, g, b):
    mu = jnp.mean(x, -1, keepdims=True)
    xc = x - mu
    var = jnp.mean(xc * xc, -1, keepdims=True)
    return xc * lax.rsqrt(var + NORM_EPS) * g + b


ADA_TN = 512
N_COND = 3


def _adaln_kernel(ct_ref, w_ref, b_ref, o_ref):
    s = _silu(ct_ref[...])
    w = w_ref[0]
    rows = [jnp.sum(s[:, r:r + 1] * w, axis=0, keepdims=True) for r in range(N_COND)]
    rows.append(jnp.zeros((8 - N_COND, w.shape[1]), F32))
    o_ref[0] = jnp.concatenate(rows, axis=0) + b_ref[0]


def _adaln(cond_t, ada_w, ada_b):
    depth, d, n = ada_w.shape
    return pl.pallas_call(
        _adaln_kernel,
        grid=(depth, n // ADA_TN),
        in_specs=[
            pl.BlockSpec((d, 8), lambda l, j: (0, 0)),
            pl.BlockSpec((1, d, ADA_TN), lambda l, j: (l, 0, j)),
            pl.BlockSpec((1, 1, ADA_TN), lambda l, j: (l, 0, j)),
        ],
        out_specs=pl.BlockSpec((1, 8, ADA_TN), lambda l, j: (l, 0, j)),
        out_shape=jax.ShapeDtypeStruct((depth, 8, n), F32),
        compiler_params=_params("arbitrary", "arbitrary"),
        name="adaln",
    )(cond_t, ada_w, ada_b.reshape(depth, 1, n))


def _even_in_kernel(x_ref, mod_ref, win_ref, qg_ref, kvg_ref, wq_ref, wqr_ref, wk_ref, wvt_ref,
                    cos_ref, sin_ref, q_ref, k_ref, vt_ref, us_ref):
    m = mod_ref[0]
    u = (x_ref[0] * (1.0 + m[1:2]) + m[0:1]).astype(BF16)
    h = _dot(u, win_ref[...])
    c0 = MLA_Q_LORA
    c1 = c0 + MLA_KV_LORA
    c2 = c1 + S5_WIDTH
    us_ref[0] = h[:, c1:c2].astype(BF16)
    cos = cos_ref[...]
    sin = sin_ref[...]
    kr = h[:, c2:c2 + LANE] * cos + h[:, c2 + LANE:c2 + 2 * LANE] * sin
    cqn = _rms(h[:, :c0], qg_ref[...]).astype(BF16)
    qa = _dot(cqn, wq_ref[...])
    qb = _dot(cqn, wqr_ref[...])
    kvn = _rms(h[:, c0:c1], kvg_ref[...]).astype(BF16)
    ka = _dot(kvn, wk_ref[...])
    vt = _dot_nt(wvt_ref[...], kvn).astype(BF16)
    ones = jnp.ones((V_ROWS - MLA_V, vt.shape[1]), BF16)
    for hh in range(MLA_HEADS):
        vt_ref[0, hh * V_ROWS:hh * V_ROWS + MLA_V] = vt[hh * MLA_V:(hh + 1) * MLA_V]
        vt_ref[0, hh * V_ROWS + MLA_V:(hh + 1) * V_ROWS] = ones
    for hh in range(MLA_HEADS):
        sl = slice(hh * HEAD_PAD, (hh + 1) * HEAD_PAD)
        q_ref[0, :, sl] = ((qa[:, sl] * cos + qb[:, sl] * sin) * Q_SCALE).astype(BF16)
        k_ref[0, :, sl] = (ka[:, sl] + kr).astype(BF16)


def _even_in(x, mod, w, cos, sin, tm):
    b, l, d = x.shape
    hp = MLA_HEADS * HEAD_PAD
    row = lambda n: pl.BlockSpec((1, tm, n), lambda bi, i: (bi, i, 0))
    return pl.pallas_call(
        _even_in_kernel,
        grid=(b, l // tm),
        in_specs=[
            row(d),
            pl.BlockSpec((1, 8, d), lambda bi, i: (bi, 0, 0)),
            _const_spec(w["w_in"].shape), _const_spec(w["q_norm"].shape), _const_spec(w["kv_norm"].shape),
            _const_spec(w["wq"].shape), _const_spec(w["wq_rot"].shape), _const_spec(w["wk"].shape),
            _const_spec(w["wv_t"].shape),
            pl.BlockSpec((tm, HEAD_PAD), lambda bi, i: (i, 0)),
            pl.BlockSpec((tm, HEAD_PAD), lambda bi, i: (i, 0)),
        ],
        out_specs=[row(hp), row(hp),
                   pl.BlockSpec((1, MLA_HEADS * V_ROWS, tm), lambda bi, i: (bi, 0, i)),
                   row(S5_WIDTH)],
        out_shape=[
            jax.ShapeDtypeStruct((b, l, hp), BF16),
            jax.ShapeDtypeStruct((b, l, hp), BF16),
            jax.ShapeDtypeStruct((b, MLA_HEADS * V_ROWS, l), BF16),
            jax.ShapeDtypeStruct((b, l, S5_WIDTH), BF16),
        ],
        compiler_params=_params("parallel", "parallel"),
        name="even_in",
    )(x, mod, w["w_in"], w["q_norm"], w["kv_norm"], w["wq"], w["wq_rot"], w["wk"], w["wv_t"], cos, sin)


ATT_TQ = 512
ATT_TK = 512
ATT_UNROLL = 6


HEADS_PER_STEP = 2
BF16_SUBLANES = 16
V_ROWS = MLA_V + BF16_SUBLANES


def _attn_kernel(q_ref, *refs, kv_lens):
    o_ref, s_buf = refs[-2:]
    kv_refs = refs[:-2]
    tq = q_ref.shape[1]
    tk = s_buf.shape[2]
    heads = range(HEADS_PER_STEP)
    q_t = [q_ref[0, :, hh * HEAD_PAD:(hh + 1) * HEAD_PAD].astype(F32).T.astype(BF16) for hh in heads]

    def fill(slot, src, start, size):
        kc = kv_refs[2 * src][0, pl.ds(start, size), :]
        tops = []
        for hh in heads:
            s = _dot(kc[:, hh * HEAD_PAD:(hh + 1) * HEAD_PAD], q_t[hh])
            s_buf[slot, hh, :size] = s
            tops.append(jnp.max(s, axis=0, keepdims=True))
        return tuple(tops)

    def drain(slot, src, start, size, tops, carry):
        vt = kv_refs[2 * src + 1][0, :, pl.ds(start, size)]
        new = []
        for hh in heads:
            m, acc = carry[hh]
            m_new = jnp.maximum(m, tops[hh])
            p = jnp.exp2(s_buf[slot, hh, :size] - m_new).astype(BF16)
            acc = jnp.exp2(m - m_new) * acc + _dot(vt[hh * V_ROWS:(hh + 1) * V_ROWS], p)
            new.append((m_new, acc))
        return tuple(new)

    carry = tuple((jnp.full((1, tq), -jnp.inf, F32), jnp.zeros((V_ROWS, tq), F32)) for _ in heads)
    chunks = [(src, j0, min(tk, n - j0)) for src, n in enumerate(kv_lens) for j0 in range(0, n, tk)]
    trips = (kv_lens[0] // tk - 1) // ATT_UNROLL if kv_lens[0] % tk == 0 else 0
    trips = trips if trips >= 2 else 0
    tops = fill(0, *chunks[0])
    if trips:
        main = lambda j: (0, pl.multiple_of(j * tk, tk), tk)

        def body(t, state):
            tops, c = state
            for i in range(ATT_UNROLL):
                j = t * ATT_UNROLL + i
                tops_next = fill((i + 1) % 2, *main(j + 1))
                c = drain(i % 2, *main(j), tops, c)
                tops = tops_next
            return tops, c

        tops, carry = lax.fori_loop(0, trips, body, (tops, carry))
    for j in range(trips * ATT_UNROLL, len(chunks)):
        tops_next = fill((j + 1) % 2, *chunks[j + 1]) if j + 1 < len(chunks) else None
        carry = drain(j % 2, *chunks[j], tops, carry)
        tops = tops_next
    out_t = jnp.concatenate([acc[:MLA_V] / acc[MLA_V:MLA_V + 1] for _, acc in carry], axis=0)
    o_ref[0] = out_t.T.astype(BF16)


def _attention(q, kvs):
    b, lq, _ = q.shape
    tq = min(ATT_TQ, lq)
    hs = HEADS_PER_STEP
    in_specs = [pl.BlockSpec((1, tq, hs * HEAD_PAD), lambda bi, hp, i: (bi, i, hp))]
    args = [q]
    for k, vt in kvs:
        n = k.shape[1]
        in_specs.append(pl.BlockSpec((1, n, hs * HEAD_PAD), lambda bi, hp, i: (bi, 0, hp)))
        in_specs.append(pl.BlockSpec((1, hs * V_ROWS, n), lambda bi, hp, i: (bi, hp, 0)))
        args += [k, vt]
    return pl.pallas_call(
        functools.partial(_attn_kernel, kv_lens=tuple(k.shape[1] for k, _ in kvs)),
        grid=(b, MLA_HEADS // hs, lq // tq),
        in_specs=in_specs,
        out_specs=pl.BlockSpec((1, tq, hs * MLA_V), lambda bi, hp, i: (bi, i, hp)),
        out_shape=jax.ShapeDtypeStruct((b, lq, MLA_HEADS * MLA_V), BF16),
        scratch_shapes=[pltpu.VMEM((2, hs, ATT_TK, tq), F32)],
        compiler_params=_params("parallel", "parallel", "parallel"),
        name="attention",
    )(*args)


def _s5_x_kernel(uc_ref, ul_ref, w_ref, xc_ref, xl_ref):
    xc_ref[...] = _dot(uc_ref[0], w_ref[0])
    xl_ref[...] = _dot(ul_ref[0], w_ref[0])


def _s5_chunk_inputs(u_ctx, u_lat, wx):
    g, _, n = u_lat.shape
    rows = lambda u: pl.BlockSpec((1, u.shape[1], n), lambda gi: (gi, 0, 0))
    cols = lambda u: pl.BlockSpec((u.shape[1], n), lambda gi: (0, gi))
    return pl.pallas_call(
        _s5_x_kernel,
        grid=(g,),
        in_specs=[rows(u_ctx), rows(u_lat), pl.BlockSpec((1, n, n), lambda gi: (gi, 0, 0))],
        out_specs=[cols(u_ctx), cols(u_lat)],
        out_shape=[jax.ShapeDtypeStruct((u.shape[1], g * n), F32) for u in (u_ctx, u_lat)],
        compiler_params=_params("parallel"),
        name="s5_chunk_inputs",
    )(u_ctx, u_lat, wx)


S5_SCAN_GROUPS = 16


def _s5_scan_kernel(xc_ref, xl_ref, lre_ref, lim_ref, sc_ref, sl_ref):
    ns = S5_STATE
    lre = lre_ref[...]
    lim = lim_ref[...]
    is_fwd = lax.broadcasted_iota(jnp.int32, lre.shape, 1) < ns

    def walk(x_ref, s_ref, carry):
        n = x_ref.shape[1]

        def step(i, carry):
            sre, sim = carry
            cf = i
            cb = n - 1 - i
            s_ref[0, cf, :, 0:ns] = sre[:, :ns]
            s_ref[0, cb, :, ns:2 * ns] = sre[:, ns:]
            s_ref[0, cf, :, 2 * ns:3 * ns] = sim[:, :ns]
            s_ref[0, cb, :, 3 * ns:] = sim[:, ns:]
            xf = x_ref[0, cf]
            xb = x_ref[0, cb]
            xre = jnp.where(is_fwd, xf[:, :2 * ns], xb[:, :2 * ns])
            xim = jnp.where(is_fwd, xf[:, 2 * ns:], xb[:, 2 * ns:])
            return sre * lre - sim * lim + xre, sre * lim + sim * lre + xim

        return lax.fori_loop(0, n, step, carry)

    zero = jnp.zeros(lre.shape, F32)
    walk(xl_ref, sl_ref, walk(xc_ref, sc_ref, (zero, zero)))


def _s5_scan(x_ctx, x_lat, p1, p2):
    b, _, g, w = x_lat.shape
    gb = S5_SCAN_GROUPS
    chunks = lambda x: pl.BlockSpec((1, x.shape[1], gb, w), lambda bi, gi: (bi, 0, gi, 0))
    pole = pl.BlockSpec((gb, w // 2), lambda bi, gi: (gi, 0))
    return pl.pallas_call(
        _s5_scan_kernel,
        grid=(b, g // gb),
        in_specs=[chunks(x_ctx), chunks(x_lat), pole, pole],
        out_specs=[chunks(x_ctx), chunks(x_lat)],
        out_shape=[jax.ShapeDtypeStruct(x.shape, F32) for x in (x_ctx, x_lat)],
        compiler_params=_params("parallel", "parallel"),
        name="s5_scan",
    )(x_ctx, x_lat, p1, p2)


def _s5_y_kernel(uc_ref, sc_ref, ul_ref, sl_ref, w_ref, yc_ref, yl_ref):
    n = ul_ref.shape[2]
    for u_ref, s_ref, y_ref in ((uc_ref, sc_ref, yc_ref), (ul_ref, sl_ref, yl_ref)):
        y = _dot(u_ref[0], w_ref[0, :n]) + _dot(s_ref[...].astype(BF16), w_ref[0, n:])
        y_ref[0] = y.astype(y_ref.dtype)


def _s5_outputs(u_ctx, s_ctx, u_lat, s_lat, wy):
    g, _, n = u_lat.shape
    rows = lambda u: pl.BlockSpec((1, u.shape[1], n), lambda gi: (gi, 0, 0))
    cols = lambda u: pl.BlockSpec((u.shape[1], n), lambda gi: (0, gi))
    return pl.pallas_call(
        _s5_y_kernel,
        grid=(g,),
        in_specs=[rows(u_ctx), cols(u_ctx), rows(u_lat), cols(u_lat),
                  pl.BlockSpec((1, 2 * n, n), lambda gi: (gi, 0, 0))],
        out_specs=[rows(u_ctx), rows(u_lat)],
        out_shape=[jax.ShapeDtypeStruct(u.shape, BF16) for u in (u_ctx, u_lat)],
        compiler_params=_params("parallel"),
        name="s5_outputs",
    )(u_ctx, s_ctx, u_lat, s_lat, wy)


def _cmul(ar, ai, br, bi):
    return ar * br - ai * bi, ar * bi + ai * br


def _s5_weights(lam_re, lam_im, log_dt, b_re, b_im, c_re, c_im):
    t = S5_CHUNK
    hi = lax.Precision.HIGHEST
    dt = jnp.exp(log_dt)[..., None]
    mag = jnp.exp(lam_re * dt)
    lb_re = mag * jnp.cos(lam_im * dt)
    lb_im = mag * jnp.sin(lam_im * dt)
    den = lam_re * lam_re + lam_im * lam_im
    nr = lb_re - 1.0
    fr = (nr * lam_re + lb_im * lam_im) / den
    fi = (lb_im * lam_re - nr * lam_im) / den
    bb_re = fr[..., None] * b_re - fi[..., None] * b_im
    bb_im = fr[..., None] * b_im + fi[..., None] * b_re
    pw = [(jnp.ones_like(lb_re), jnp.zeros_like(lb_im))]
    for _ in range(t):
        pw.append(_cmul(pw[-1][0], pw[-1][1], lb_re, lb_im))
    pw_re = jnp.stack([p[0] for p in pw])
    pw_im = jnp.stack([p[1] for p in pw])
    pb_re, pb_im = _cmul(pw_re[:t, ..., None], pw_im[:t, ..., None], bb_re[None], bb_im[None])
    kern = (jnp.einsum('dgpn,ldgnq->ldgpq', c_re, pb_re, precision=hi)
            - jnp.einsum('dgpn,ldgnq->ldgpq', c_im, pb_im, precision=hi))
    lag = jnp.arange(t)[None, :] - jnp.arange(t)[:, None]
    pick = lambda d: (d[..., None] == jnp.arange(t)).astype(F32)
    kf = jnp.einsum('stl,lgpq->stgpq', pick(lag), kern[:, 0], precision=hi)
    kb = jnp.einsum('stl,lgpq->stgpq', pick(-lag), kern[:, 1], precision=hi)
    g = lam_re.shape[1]
    n_lane = t * S5_GROUP
    toep = (kf + kb).transpose(2, 0, 4, 1, 3).reshape(g, n_lane, n_lane)

    def to_state(w):
        return w.transpose(1, 0, 3, 2).reshape(g, n_lane, S5_STATE)

    wx = jnp.concatenate([to_state(pb_re[::-1, 0]), to_state(pb_re[:, 1]),
                          to_state(pb_im[::-1, 0]), to_state(pb_im[:, 1])], axis=-1)

    def from_state(d, p_re, p_im):
        cr = c_re[d][None]
        ci = c_im[d][None]
        a = cr * p_re[:, :, None, :] - ci * p_im[:, :, None, :]
        bneg = -(cr * p_im[:, :, None, :] + ci * p_re[:, :, None, :])
        lanes = lambda w: w.transpose(1, 3, 0, 2).reshape(g, S5_STATE, n_lane)
        return lanes(a), lanes(bneg)

    f_re, f_im = from_state(0, pw_re[1:, 0], pw_im[1:, 0])
    b_re, b_im = from_state(1, pw_re[t:0:-1, 1], pw_im[t:0:-1, 1])
    wy = jnp.concatenate([toep, f_re, b_re, f_im, b_im], axis=1)
    lt_re, lt_im = pw_re[t], pw_im[t]
    p1 = jnp.concatenate([lt_re[0], lt_re[1]], axis=-1)
    p2 = jnp.concatenate([lt_im[0], lt_im[1]], axis=-1)
    return wx.astype(BF16), wy.astype(BF16), p1, p2


S5_RELAYOUT_CHUNKS = 64
LANE_GROUPS = LANE // S5_GROUP


def _to_groups_kernel(us_ref, u_ref, tok_ref):
    t, p = S5_CHUNK, S5_GROUP
    nc = u_ref.shape[1]
    tiles = S5_WIDTH // LANE
    for q in range(tiles):
        tok_ref[q] = us_ref[0, :, q * LANE:(q + 1) * LANE].astype(F32)
    for s in range(t):
        si = s % LANE_GROUPS
        for q in range(tiles):
            piece = tok_ref[q, pl.ds(s, nc, stride=t), :]
            for gi in range(LANE_GROUPS):
                shift = ((si - gi) * p) % LANE
                moved = pltpu.roll(piece, shift, 1) if shift else piece
                u_ref[q * LANE_GROUPS + gi, :, s * p:(s + 1) * p] = moved[:, si * p:(si + 1) * p].astype(BF16)


def _from_groups_kernel(y_ref, o_ref, step_ref, tok_ref):
    t, p = S5_CHUNK, S5_GROUP
    nc = y_ref.shape[1]
    for g in range(S5_GROUPS):
        q, gi = divmod(g, LANE_GROUPS)
        for so in range(t * p // LANE):
            piece = y_ref[g, :, so * LANE:(so + 1) * LANE].astype(F32)
            for si in range(LANE_GROUPS):
                shift = ((gi - si) * p) % LANE
                moved = pltpu.roll(piece, shift, 1) if shift else piece
                step_ref[so * LANE_GROUPS + si, q, :, gi * p:(gi + 1) * p] = moved[:, gi * p:(gi + 1) * p]
    for q in range(S5_WIDTH // LANE):
        for s in range(t):
            tok_ref[q, pl.ds(s, nc, stride=t), :] = step_ref[s, q]
        o_ref[0, :, q * LANE:(q + 1) * LANE] = tok_ref[q].astype(o_ref.dtype)


def _relayout_tile(n_chunks):
    return min(S5_RELAYOUT_CHUNKS, n_chunks)


def _s5_to_groups(us):
    b, n, w = us.shape
    t = S5_CHUNK
    nc = _relayout_tile(n // t)
    steps = n // t // nc
    return pl.pallas_call(
        _to_groups_kernel,
        grid=(b, steps),
        in_specs=[pl.BlockSpec((1, nc * t, w), lambda bi, i: (bi, i, 0))],
        out_specs=pl.BlockSpec((S5_GROUPS, nc, t * S5_GROUP), lambda bi, i: (0, bi * steps + i, 0)),
        out_shape=jax.ShapeDtypeStruct((S5_GROUPS, b * (n // t), t * S5_GROUP), BF16),
        scratch_shapes=[pltpu.VMEM((w // LANE, nc * t, LANE), F32)],
        compiler_params=_params("parallel", "parallel"),
        name="s5_to_groups",
    )(us)


def _s5_from_groups(y, b):
    g, r, tp = y.shape
    t = S5_CHUNK
    n = r // b * t
    nc = _relayout_tile(n // t)
    steps = n // t // nc
    w = g * S5_GROUP
    return pl.pallas_call(
        _from_groups_kernel,
        grid=(b, steps),
        in_specs=[pl.BlockSpec((g, nc, tp), lambda bi, i: (0, bi * steps + i, 0))],
        out_specs=pl.BlockSpec((1, nc * t, w), lambda bi, i: (bi, i, 0)),
        out_shape=jax.ShapeDtypeStruct((b, n, w), BF16),
        scratch_shapes=[pltpu.VMEM((t, w // LANE, nc, LANE), F32), pltpu.VMEM((w // LANE, nc * t, LANE), F32)],
        compiler_params=_params("parallel", "parallel"),
        name="s5_from_groups",
    )(y)


def _s5_layer(us_ctx, us_lat, w):
    b = us_ctx.shape[0]
    g = S5_GROUPS
    u_c, u_l = _s5_to_groups(us_ctx), _s5_to_groups(us_lat)
    x_c, x_l = _s5_chunk_inputs(u_c, u_l, w["wx"])
    as_chunks = lambda x: x.reshape(b, x.shape[0] // b, g, 2 * LANE)
    s_c, s_l = _s5_scan(as_chunks(x_c), as_chunks(x_l), w["p1"], w["p2"])
    y_c, y_l = _s5_outputs(u_c, s_c.reshape(x_c.shape), u_l, s_l.reshape(x_l.shape), w["wy"])
    return _s5_from_groups(y_c, b), _s5_from_groups(y_l, b)


FFN_CHUNK = FFN_HIDDEN // 2


def _mid_tail(x, y, m, wffn_ref, lng_ref, lnb_ref, x1_ref, a_ref, gt_ref):
    x1 = _layer_norm(DN_ALPHA * x + m[2:3] * y, lng_ref[...], lnb_ref[...])
    x1_ref[0] = x1
    uf = (x1 * (1.0 + m[4:5]) + m[3:4]).astype(BF16)
    for j in range(FFN_HIDDEN // FFN_CHUNK):
        sl = slice(j * FFN_CHUNK, (j + 1) * FFN_CHUNK)
        a_ref[0, :, sl] = _dot(uf, wffn_ref[:, sl]).astype(BF16)
        gsl = slice(FFN_HIDDEN + j * FFN_CHUNK, FFN_HIDDEN + (j + 1) * FFN_CHUNK)
        gt_ref[0, :, sl] = _dot(uf, wffn_ref[:, gsl]).astype(BF16)


def _gelu_tanh(x):
    return 0.5 * x * (1.0 + jnp.tanh(math.sqrt(2.0 / math.pi) * (x + 0.044715 * x * x * x)))


def _even_mid_kernel(x_ref, mod_ref, att_ref, ys_ref, us_ref, d_ref, wglu_ref, wout_ref, lng_ref, lnb_ref,
                     wffn_ref, x1_ref, a_ref, gt_ref):
    z = _gelu_tanh(ys_ref[0].astype(F32) + d_ref[...] * us_ref[0].astype(F32))
    s5 = z * _sigmoid(_dot(z.astype(BF16), wglu_ref[...]))
    n_att = att_ref.shape[2]
    y = _dot(att_ref[0], wout_ref[:n_att]) + _dot(s5.astype(BF16), wout_ref[n_att:])
    _mid_tail(x_ref[0], y, mod_ref[0], wffn_ref, lng_ref, lnb_ref, x1_ref, a_ref, gt_ref)


def _odd_mid_kernel(x_ref, mod_ref, of_ref, ob_ref, g_ref, hn_ref, wout_ref, lng_ref, lnb_ref,
                    wffn_ref, x1_ref, a_ref, gt_ref):
    o = of_ref[0] + ob_ref[0]
    g = g_ref[0]
    gate = g * _sigmoid(g)
    parts = []
    for hh in range(HG_HEADS):
        sl = slice(hh * HG_DV, (hh + 1) * HG_DV)
        parts.append((_rms(o[:, sl], hn_ref[...]) * gate[:, sl]).astype(BF16))
    y = _dot(jnp.concatenate(parts, axis=-1), wout_ref[...])
    _mid_tail(x_ref[0], y, mod_ref[0], wffn_ref, lng_ref, lnb_ref, x1_ref, a_ref, gt_ref)


MID_TM = 512


def _mid_call(kernel, name, x, mod, row_args, row_specs, const_args):
    b, l, d = x.shape
    tm = min(MID_TM, l)
    row = lambda n: pl.BlockSpec((1, tm, n), lambda bi, i: (bi, i, 0))
    in_specs = [row(d), pl.BlockSpec((1, 8, d), lambda bi, i: (bi, 0, 0))]
    in_specs += [spec(tm) for spec in row_specs]
    in_specs += [_const_spec(a.shape) for a in const_args]
    return pl.pallas_call(
        kernel,
        grid=(b, l // tm),
        in_specs=in_specs,
        out_specs=[row(d), row(FFN_HIDDEN), row(FFN_HIDDEN)],
        out_shape=[jax.ShapeDtypeStruct((b, l, d), F32),
                   jax.ShapeDtypeStruct((b, l, FFN_HIDDEN), BF16),
                   jax.ShapeDtypeStruct((b, l, FFN_HIDDEN), BF16)],
        compiler_params=_params("parallel", "parallel"),
        name=name,
    )(x, mod, *row_args, *const_args)


def _row_spec(n, lane_block=0):
    return lambda tm: pl.BlockSpec((1, tm, n), lambda bi, i: (bi, i, lane_block))


def _even_mid(x, mod, att, ys, us, w):
    return _mid_call(
        _even_mid_kernel, "even_mid", x, mod, [att, ys, us],
        [_row_spec(att.shape[2]), _row_spec(S5_WIDTH), _row_spec(S5_WIDTH)],
        [w["s5_d"], w["w_glu"], w["w_out"], w["ln_g0"], w["ln_b0"], w["ffn_w_in"]])


def _odd_mid(x, mod, o_f, o_b, hg, w):
    return _mid_call(
        _odd_mid_kernel, "odd_mid", x, mod, [o_f, o_b, hg],
        [_row_spec(HG_WIDTH), _row_spec(HG_WIDTH), _row_spec(HG_WIDTH, 4)],
        [w["hg_norm"], w["w_out"], w["ln_g0"], w["ln_b0"], w["ffn_w_in"]])


FFN_TM = 256
HALO = 8


def _ffn_out_kernel(a_ref, ap_ref, an_ref, gt_ref, x1_ref, mod_ref, cw_ref, cb_ref, wout_ref, lng_ref, lnb_ref,
                    o_ref):
    i = pl.program_id(1)
    tm = a_ref.shape[1]
    a = a_ref[0].astype(F32)
    row = lax.broadcasted_iota(jnp.int32, (HALO, 1), 0)
    before = jnp.where(i == 0, 0.0, ap_ref[0, HALO - 1:HALO, :].astype(F32))
    after = jnp.where(i == pl.num_programs(1) - 1, 0.0, an_ref[0, 0:1, :].astype(F32))
    a_prev = pltpu.roll(a, 1, 0)
    a_prev = jnp.concatenate([jnp.where(row == 0, before, a_prev[:HALO]), a_prev[HALO:]], axis=0)
    a_next = pltpu.roll(a, tm - 1, 0)
    a_next = jnp.concatenate([a_next[:tm - HALO], jnp.where(row == HALO - 1, after, a_next[tm - HALO:])], axis=0)
    cw = cw_ref[...]
    conv = cb_ref[...] + cw[0:1] * a_prev + cw[1:2] * a + cw[2:3] * a_next
    hidden = (_silu(conv) * gt_ref[0].astype(F32)).astype(BF16)
    f = _dot(hidden, wout_ref[...])
    m = mod_ref[0]
    o_ref[0] = _layer_norm(DN_ALPHA * x1_ref[0] + m[5:6] * f, lng_ref[...], lnb_ref[...])


def _ffn_out(a, gt, x1, mod, w):
    b, l, d = x1.shape
    tm = min(FFN_TM, l)
    per = tm // HALO
    last = l // HALO - 1
    row = lambda n: pl.BlockSpec((1, tm, n), lambda bi, i: (bi, i, 0))
    consts = [w["conv_w"], w["conv_b"], w["ffn_w_out"], w["ln_g1"], w["ln_b1"]]
    return pl.pallas_call(
        _ffn_out_kernel,
        grid=(b, l // tm),
        in_specs=[
            row(FFN_HIDDEN),
            pl.BlockSpec((1, HALO, FFN_HIDDEN), lambda bi, i: (bi, jnp.maximum(i * per - 1, 0), 0)),
            pl.BlockSpec((1, HALO, FFN_HIDDEN), lambda bi, i: (bi, jnp.minimum((i + 1) * per, last), 0)),
            row(FFN_HIDDEN), row(d),
            pl.BlockSpec((1, 8, d), lambda bi, i: (bi, 0, 0)),
        ] + [_const_spec(c.shape) for c in consts],
        out_specs=row(d),
        out_shape=jax.ShapeDtypeStruct((b, l, d), F32),
        compiler_params=_params("parallel", "parallel"),
        name="ffn_out",
    )(a, a, a, gt, x1, mod, *consts)


ODD_TM = 256


def _odd_in_kernel(x_ref, mod_ref, w_ref, o_ref):
    m = mod_ref[0]
    u = (x_ref[0] * (1.0 + m[1:2]) + m[0:1]).astype(BF16)
    for j in range(HG_IN // HG_WIDTH):
        sl = slice(j * HG_WIDTH, (j + 1) * HG_WIDTH)
        o_ref[0, :, sl] = _dot(u, w_ref[:, sl])


def _odd_in(x, mod, w_in):
    b, l, d = x.shape
    tm = min(ODD_TM, l)
    return pl.pallas_call(
        _odd_in_kernel,
        grid=(b, l // tm),
        in_specs=[pl.BlockSpec((1, tm, d), lambda bi, i: (bi, i, 0)),
                  pl.BlockSpec((1, 8, d), lambda bi, i: (bi, 0, 0)),
                  _const_spec(w_in.shape)],
        out_specs=pl.BlockSpec((1, tm, HG_IN), lambda bi, i: (bi, i, 0)),
        out_shape=jax.ShapeDtypeStruct((b, l, HG_IN), F32),
        compiler_params=_params("parallel", "parallel"),
        name="odd_in",
    )(x, mod, w_in)


def _hgrn_direction(q, pre, v, lb, state, level, reverse):
    c = q.shape[0]
    row = lax.broadcasted_iota(jnp.int32, q.shape, 0)
    pos = (c - 1 - row) if reverse else row

    def prev(x, k):
        return pltpu.roll(x, (c - k) if reverse else k, 0)

    def nxt(x, k):
        return pltpu.roll(x, k if reverse else (c - k), 0)

    f = lb + (1.0 - lb) * _sigmoid(pre)
    key = 1.0 - f
    cum = jnp.log(f)
    k = 1
    while k < c:
        cum = cum + jnp.where(pos >= k, prev(cum, k), 0.0)
        k *= 2
    last = cum[0:1] if reverse else cum[c - 1:c]

    qb = q.astype(BF16)
    kb = key.astype(BF16)
    scores = jnp.where(level == 0, _dot_nt(qb, kb), 0.0)
    z = cum
    half = 1
    lvl = 1
    while half < c:
        ref = jnp.where((pos & half) == 0, z, prev(z, half))
        e = jnp.exp(-jnp.abs(cum - ref))
        x = (jnp.where((pos & half) != 0, q, key) * e).astype(BF16)
        scores = jnp.where(level == lvl, _dot_nt(x, x), scores)
        z = jnp.where((pos & half) != 0, z, nxt(z, half))
        half *= 2
        lvl += 1

    o = _dot(scores.astype(BF16), v)
    o = o + _dot_nt((q * jnp.exp(cum)).astype(BF16), state.astype(BF16))
    kd = (key * jnp.exp(last - cum)).astype(BF16)
    new_state = jnp.exp(last) * state + _dot_tn(v, kd)
    return o, new_state


def _hgrn_kernel(qf_ref, pf_ref, vf_ref, qb_ref, pb_ref, vb_ref, lb_ref, lvf_ref, lvb_ref, s0_ref,
                 of_ref, ob_ref, sfin_ref, st_ref):
    ci = pl.program_id(2)

    @pl.when(ci == 0)
    def _():
        st_ref[...] = s0_ref[0, 0]

    o, s = _hgrn_direction(qf_ref[0], pf_ref[0], vf_ref[0].astype(BF16), lb_ref[0], st_ref[0],
                           lvf_ref[...], False)
    of_ref[0] = o
    st_ref[0] = s
    o, s = _hgrn_direction(qb_ref[0], pb_ref[0], vb_ref[0].astype(BF16), lb_ref[1], st_ref[1],
                           lvb_ref[...], True)
    ob_ref[0] = o
    st_ref[1] = s

    @pl.when(ci == pl.num_programs(2) - 1)
    def _():
        sfin_ref[0, 0] = st_ref[...]


def _hgrn_levels(c):
    p = jnp.arange(c, dtype=jnp.int32)
    x = p[:, None] ^ p[None, :]
    lvl = jnp.where(x == 0, 0, 32 - lax.clz(x))
    fwd = jnp.where(p[:, None] >= p[None, :], lvl, -1)
    bwd = jnp.where(p[:, None] <= p[None, :], lvl, -1)
    return fwd.astype(jnp.int32), bwd.astype(jnp.int32)


def _hgrn(hg, lb, s0):
    b, l, _ = hg.shape
    c = HG_CHUNK
    n = l // c
    nh = HG_HEADS
    lvf, lvb = _hgrn_levels(c)
    fw = lambda blk: pl.BlockSpec((1, c, HG_DK), lambda bi, h, ci: (bi, ci, blk * nh + h))
    bw = lambda blk: pl.BlockSpec((1, c, HG_DK), lambda bi, h, ci: (bi, n - 1 - ci, blk * nh + h))
    state_spec = pl.BlockSpec((1, 1, 2, HG_DV, HG_DK), lambda bi, h, ci: (bi, h, 0, 0, 0))
    return pl.pallas_call(
        _hgrn_kernel,
        grid=(b, nh, n),
        in_specs=[fw(0), fw(1), fw(3), bw(0), bw(2), bw(3),
                  pl.BlockSpec((2, 1, HG_DK), lambda bi, h, ci: (0, 0, h)),
                  pl.BlockSpec((c, c), lambda bi, h, ci: (0, 0)),
                  pl.BlockSpec((c, c), lambda bi, h, ci: (0, 0)),
                  state_spec],
        out_specs=[pl.BlockSpec((1, c, HG_DV), lambda bi, h, ci: (bi, ci, h)),
                   pl.BlockSpec((1, c, HG_DV), lambda bi, h, ci: (bi, n - 1 - ci, h)),
                   state_spec],
        out_shape=[jax.ShapeDtypeStruct((b, l, HG_WIDTH), F32),
                   jax.ShapeDtypeStruct((b, l, HG_WIDTH), F32),
                   jax.ShapeDtypeStruct((b, nh, 2, HG_DV, HG_DK), F32)],
        scratch_shapes=[pltpu.VMEM((2, HG_DV, HG_DK), F32)],
        compiler_params=_params("parallel", "parallel", "arbitrary"),
        name="hgrn",
    )(hg, hg, hg, hg, hg, hg, lb, lvf, lvb, s0)


HG_DIRECT_CHUNK = 64
HG_DIRECT_BLOCK = 1024
HG_DIRECT_MAX_EXPONENT = 60.0
HG_DIRECT_MIN_LB = math.exp(-HG_DIRECT_MAX_EXPONENT / (HG_DIRECT_CHUNK // 2))


def _split3(x):
    hi = x.astype(BF16)
    r = x - hi.astype(F32)
    mid = r.astype(BF16)
    lo = (r - mid.astype(F32)).astype(BF16)
    return hi, mid, lo


def _hgrn_direct_block(q_ref, p_ref, v_ref, o_ref, lb, state, tri3, visible, reverse):
    c = HG_DIRECT_CHUNK
    n_sub = q_ref.shape[1] // c
    half = c // 2
    order = list(range(n_sub - 1, -1, -1) if reverse else range(n_sub))
    rows = [slice(j * c, (j + 1) * c) for j in order]
    fs = [lb + (1.0 - lb) * _sigmoid(p_ref[0, r, :]) for r in rows]
    cums = [_dot(tri3, jnp.concatenate(_split3(jnp.log(f)), axis=0)) for f in fs]
    mids = [cum[half:half + 1] if reverse else cum[half - 1:half] for cum in cums]
    lasts = [cum[0:1] if reverse else cum[c - 1:c] for cum in cums]
    qts = [(q_ref[0, r, :] * jnp.exp(cum - mid)).astype(BF16) for r, cum, mid in zip(rows, cums, mids)]
    kts = [((1.0 - f) * jnp.exp(mid - cum)).astype(BF16) for f, cum, mid in zip(fs, cums, mids)]
    vs = [v_ref[0, r, :].astype(BF16) for r in rows]
    scores = [jnp.where(visible, _dot_nt(qt, kt), 0.0).astype(BF16) for qt, kt in zip(qts, kts)]
    adds = [jnp.exp(last - mid) * _dot_tn(v, kt) for last, mid, v, kt in zip(lasts, mids, vs, kts)]
    intra = [_dot(sc, v) for sc, v in zip(scores, vs)]
    entering = []
    for mid, last, add in zip(mids, lasts, adds):
        entering.append((state * jnp.exp(mid)).astype(BF16))
        state = jnp.exp(last) * state + add
    for r, qt, o, s_in in zip(rows, qts, intra, entering):
        o_ref[0, r, :] = o + _dot_nt(qt, s_in)
    return state


def _hgrn_direct_kernel(qf_ref, pf_ref, vf_ref, qb_ref, pb_ref, vb_ref, lb_ref, trif_ref, trib_ref, s0_ref,
                        of_ref, ob_ref, sfin_ref, st_ref):
    ci = pl.program_id(2)
    c = HG_DIRECT_CHUNK

    @pl.when(ci == 0)
    def _():
        st_ref[...] = s0_ref[0, 0]

    row = lax.broadcasted_iota(jnp.int32, (c, c), 0)
    col = lax.broadcasted_iota(jnp.int32, (c, c), 1)
    st_ref[0] = _hgrn_direct_block(qf_ref, pf_ref, vf_ref, of_ref, lb_ref[0], st_ref[0], trif_ref[...],
                                   row >= col, False)
    st_ref[1] = _hgrn_direct_block(qb_ref, pb_ref, vb_ref, ob_ref, lb_ref[1], st_ref[1], trib_ref[...],
                                   row <= col, True)

    @pl.when(ci == pl.num_programs(2) - 1)
    def _():
        sfin_ref[0, 0] = st_ref[...]


def _hgrn_direct(hg, lb, s0):
    b, l, _ = hg.shape
    c = HG_DIRECT_CHUNK
    blk_rows = min(HG_DIRECT_BLOCK, l)
    n = l // blk_rows
    nh = HG_HEADS
    p = jnp.arange(c)
    lower_tri = (p[:, None] >= p[None, :]).astype(BF16)
    trif = jnp.concatenate([lower_tri] * 3, axis=1)
    trib = jnp.concatenate([lower_tri.T] * 3, axis=1)
    fw = lambda blk: pl.BlockSpec((1, blk_rows, HG_DK), lambda bi, h, ci: (bi, ci, blk * nh + h))
    bw = lambda blk: pl.BlockSpec((1, blk_rows, HG_DK), lambda bi, h, ci: (bi, n - 1 - ci, blk * nh + h))
    state_spec = pl.BlockSpec((1, 1, 2, HG_DV, HG_DK), lambda bi, h, ci: (bi, h, 0, 0, 0))
    return pl.pallas_call(
        _hgrn_direct_kernel,
        grid=(b, nh, n),
        in_specs=[fw(0), fw(1), fw(3), bw(0), bw(2), bw(3),
                  pl.BlockSpec((2, 1, HG_DK), lambda bi, h, ci: (0, 0, h)),
                  pl.BlockSpec((c, 3 * c), lambda bi, h, ci: (0, 0)),
                  pl.BlockSpec((c, 3 * c), lambda bi, h, ci: (0, 0)),
                  state_spec],
        out_specs=[pl.BlockSpec((1, blk_rows, HG_DV), lambda bi, h, ci: (bi, ci, h)),
                   pl.BlockSpec((1, blk_rows, HG_DV), lambda bi, h, ci: (bi, n - 1 - ci, h)),
                   state_spec],
        out_shape=[jax.ShapeDtypeStruct((b, l, HG_WIDTH), F32),
                   jax.ShapeDtypeStruct((b, l, HG_WIDTH), F32),
                   jax.ShapeDtypeStruct((b, nh, 2, HG_DV, HG_DK), F32)],
        scratch_shapes=[pltpu.VMEM((2, HG_DV, HG_DK), F32)],
        compiler_params=_params("parallel", "parallel", "arbitrary"),
        name="hgrn_direct",
    )(hg, hg, hg, hg, hg, hg, lb, trif, trib, s0)


def _hgrn_auto(hg, lb, s0):
    return lax.cond(jnp.min(lb) >= HG_DIRECT_MIN_LB,
                    lambda: tuple(_hgrn_direct(hg, lb, s0)), lambda: tuple(_hgrn(hg, lb, s0)))


def _rope_tables(length):
    rows = length // GRID_W
    row = jnp.repeat(jnp.arange(rows, dtype=F32), GRID_W)
    col = jnp.tile(jnp.arange(GRID_W, dtype=F32), rows)
    n_freq = MLA_ROPE // 4
    inv = ROPE_BASE ** (-jnp.arange(n_freq, dtype=F32) / n_freq)
    ar = row[:, None] * inv
    ac = col[:, None] * inv
    ang = jnp.concatenate([ar, ar, ac, ac], axis=-1)
    return jnp.cos(ang), jnp.sin(ang)


def _head_tables(cos, sin):
    n = cos.shape[0]
    pad = jnp.zeros((n, HEAD_PAD - MLA_NOPE - MLA_ROPE), F32)
    return (jnp.concatenate([jnp.ones((n, MLA_NOPE), F32), cos, pad], -1),
            jnp.concatenate([jnp.zeros((n, MLA_NOPE), F32), sin, pad], -1))


def _rotate_cols(w):
    ws = w.reshape(w.shape[:-1] + (2, 2, MLA_ROPE // 4))
    return jnp.stack([-ws[..., 1, :], ws[..., 0, :]], axis=-2).reshape(w.shape)


def _even_weights(ev_w_in, q_norm, w_uq, kv_norm, w_ukv):
    c0 = MLA_Q_LORA
    c1 = c0 + MLA_KV_LORA
    c2 = c1 + MLA_ROPE
    d = ev_w_in.shape[0]
    w_kr = ev_w_in[:, c1:c2]
    lo = jnp.zeros((d, MLA_NOPE), F32)
    hi = jnp.zeros((d, HEAD_PAD - MLA_NOPE - MLA_ROPE), F32)
    w_in = jnp.concatenate([ev_w_in[:, :c1], ev_w_in[:, c2:], lo, w_kr, hi, lo, _rotate_cols(w_kr), hi], axis=-1)
    wq = w_uq.reshape(c0, MLA_HEADS, MLA_NOPE + MLA_ROPE)
    zq = jnp.zeros((c0, MLA_HEADS, HEAD_PAD - MLA_NOPE - MLA_ROPE), F32)
    wq_pad = jnp.concatenate([wq, zq], -1).reshape(c0, MLA_HEADS * HEAD_PAD)
    wq_rot = jnp.concatenate([jnp.zeros((c0, MLA_HEADS, MLA_NOPE), F32), _rotate_cols(wq[..., MLA_NOPE:]), zq],
                             -1).reshape(c0, MLA_HEADS * HEAD_PAD)
    wkv = w_ukv.reshape(MLA_KV_LORA, MLA_HEADS, MLA_NOPE + MLA_V)
    wk = jnp.concatenate([wkv[..., :MLA_NOPE], jnp.zeros((MLA_KV_LORA, MLA_HEADS, HEAD_PAD - MLA_NOPE), F32)],
                         -1).reshape(MLA_KV_LORA, MLA_HEADS * HEAD_PAD)
    wv_t = wkv[..., MLA_NOPE:].reshape(MLA_KV_LORA, MLA_HEADS * MLA_V).T
    return {"w_in": w_in.astype(BF16), "q_norm": q_norm.reshape(1, -1), "kv_norm": kv_norm.reshape(1, -1),
            "wq": wq_pad.astype(BF16), "wq_rot": wq_rot.astype(BF16), "wk": wk.astype(BF16),
            "wv_t": wv_t.astype(BF16)}


def _layer_mod(mod, layer, b):
    d = D_MODEL
    rows = mod[layer, :, :6 * d].reshape(8, 6, d)
    pad = jnp.zeros((2, d), F32)
    lat = jnp.stack([jnp.concatenate([rows[bi], pad], 0) for bi in range(b)])
    ctx = jnp.broadcast_to(jnp.concatenate([rows[b], pad], 0)[None], (b, 8, d))
    return lat, ctx


def kernel(x, c, ctx, c_ctx, ada_w, ada_b, ln_g, ln_b, ffn_w_in, ffn_conv_w, ffn_conv_b, ffn_w_out, ev_w_in,
           mla_q_norm, mla_w_uq, mla_kv_norm, mla_w_ukv, s5_lam_re, s5_lam_im, s5_log_dt, s5_b_re, s5_b_im,
           s5_c_re, s5_c_im, s5_d, s5_w_glu, ev_w_out, hg_w_in, hg_lb, hg_norm, hg_w_out):
    b, l, d = x.shape
    lc = ctx.shape[1]
    assert b + 1 == N_COND and d == D_MODEL and l % 512 == 0 and lc % 256 == 0

    cond_t = jnp.concatenate([c, c_ctx[None], jnp.zeros((8 - N_COND, d), F32)], 0).T
    mod = _adaln(cond_t, ada_w, ada_b)

    def ffn_weights(layer):
        return {"ln_g0": ln_g[layer, 0].reshape(1, d), "ln_b0": ln_b[layer, 0].reshape(1, d),
                "ln_g1": ln_g[layer, 1].reshape(1, d), "ln_b1": ln_b[layer, 1].reshape(1, d),
                "ffn_w_in": ffn_w_in[layer].astype(BF16),
                "conv_w": jnp.concatenate([ffn_conv_w[layer], jnp.zeros((5, FFN_HIDDEN), F32)], 0),
                "conv_b": ffn_conv_b[layer].reshape(1, FFN_HIDDEN),
                "ffn_w_out": ffn_w_out[layer].astype(BF16)}

    mod_lat, mod_ctx = _layer_mod(mod, 0, b)
    we = _even_weights(ev_w_in[0], mla_q_norm[0], mla_w_uq[0], mla_kv_norm[0], mla_w_ukv[0])
    cos_l, sin_l = _head_tables(*_rope_tables(l))
    cos_c, sin_c = _head_tables(jnp.ones((lc, MLA_ROPE), F32), jnp.zeros((lc, MLA_ROPE), F32))
    q_l, k_l, v_l, us_l = _even_in(x, mod_lat, we, cos_l, sin_l, min(512, l))
    q_c, k_c, v_c, us_c = _even_in(ctx, mod_ctx, we, cos_c, sin_c, min(512, lc))
    att_l = _attention(q_l, [(k_l, v_l), (k_c, v_c)])
    att_c = _attention(q_c, [(k_c, v_c)])
    wx, wy, p1, p2 = _s5_weights(s5_lam_re[0], s5_lam_im[0], s5_log_dt[0], s5_b_re[0], s5_b_im[0],
                                 s5_c_re[0], s5_c_im[0])
    ys_c, ys_l = _s5_layer(us_c, us_l, {"wx": wx, "wy": wy, "p1": p1, "p2": p2})
    w0 = ffn_weights(0)
    w0.update({"s5_d": s5_d[0].reshape(1, S5_WIDTH), "w_glu": s5_w_glu[0].astype(BF16),
               "w_out": ev_w_out[0].astype(BF16)})
    x1, a, gt = _even_mid(x, mod_lat, att_l, ys_l, us_l, w0)
    x = _ffn_out(a, gt, x1, mod_lat, w0)
    c1, a, gt = _even_mid(ctx, mod_ctx, att_c, ys_c, us_c, w0)
    ctx = _ffn_out(a, gt, c1, mod_ctx, w0)

    mod_lat, mod_ctx = _layer_mod(mod, 1, b)
    sm = jax.nn.softmax(hg_lb, axis=0)
    lower = (jnp.cumsum(sm, axis=0) - sm[0])[1].reshape(2, 1, HG_WIDTH)
    w_hg = hg_w_in[0].astype(BF16)
    hg_c = _odd_in(ctx, mod_ctx, w_hg)
    hg_l = _odd_in(x, mod_lat, w_hg)
    zero_state = jnp.zeros((b, HG_HEADS, 2, HG_DV, HG_DK), F32)
    _, _, s_ctx = _hgrn_auto(hg_c, lower, zero_state)
    o_f, o_b, _ = _hgrn_auto(hg_l, lower, s_ctx)
    w1 = ffn_weights(1)
    w1.update({"hg_norm": hg_norm[0].reshape(1, HG_DV), "w_out": hg_w_out[0].astype(BF16)})
    x1, a, gt = _odd_mid(x, mod_lat, o_f, o_b, hg_l, w1)
    return _ffn_out(a, gt, x1, mod_lat, w1)
```

```python
import functools
import math

import jax
import jax.numpy as jnp
import numpy as np
from jax import lax
from jax.experimental import pallas as pl
from jax.experimental.pallas import tpu as pltpu

F32 = jnp.float32
BF16 = jnp.bfloat16

D_MODEL = 1024
DEPTH = 2
GRID_W = 64
NORM_EPS = 1e-6
DN_ALPHA = (2.0 * DEPTH) ** 0.25

MLA_HEADS = 8
MLA_NOPE = 64
MLA_ROPE = 32
MLA_V = 64
MLA_Q_LORA = 384
MLA_KV_LORA = 256
MLA_SCALE = (MLA_NOPE + MLA_ROPE) ** -0.5
Q_SCALE = MLA_SCALE * math.log2(math.e)
ROPE_BASE = 10000.0

S5_WIDTH = 512
S5_GROUP = 16
S5_GROUPS = S5_WIDTH // S5_GROUP
S5_STATE = 64
S5_CHUNK = 16

HG_HEADS = 8
HG_DK = 128
HG_DV = 128
HG_WIDTH = HG_HEADS * HG_DK
HG_IN = 5 * HG_WIDTH
HG_CHUNK = 128

FFN_HIDDEN = 2816

LANE = 128
HEAD_PAD = 128
EVEN_IN_PAD = MLA_Q_LORA + MLA_KV_LORA + S5_WIDTH + 2 * LANE

VMEM_LIMIT = 56 * 1024 * 1024


def _params(*sem):
    return pltpu.CompilerParams(dimension_semantics=sem, vmem_limit_bytes=VMEM_LIMIT)


def _const_spec(shape):
    zeros = (0,) * len(shape)
    return pl.BlockSpec(shape, lambda *_: zeros, pipeline_mode=pl.Buffered(1))


def _dot(a, b):
    return jnp.dot(a, b, preferred_element_type=F32)


def _dot_nt(a, b):
    return lax.dot_general(a, b, (((1,), (1,)), ((), ())), preferred_element_type=F32)


def _dot_tn(a, b):
    return lax.dot_general(a, b, (((0,), (0,)), ((), ())), preferred_element_type=F32)


def _sigmoid(x):
    return 1.0 / (1.0 + jnp.exp2(x * -math.log2(math.e)))


def _silu(x):
    return x * _sigmoid(x)


def _rms(x, g):
    return x * lax.rsqrt(jnp.mean(x * x, -1, keepdims=True) + NORM_EPS) * g


def _layer_norm(x, g, b):
    mu = jnp.mean(x, -1, keepdims=True)
    xc = x - mu
    var = jnp.mean(xc * xc, -1, keepdims=True)
    return xc * lax.rsqrt(var + NORM_EPS) * g + b


ADA_TN = 512
N_COND = 3


def _adaln_kernel(ct_ref, w_ref, b_ref, o_ref):
    s = _silu(ct_ref[...])
    w = w_ref[0]
    rows = [jnp.sum(s[:, r:r + 1] * w, axis=0, keepdims=True) for r in range(N_COND)]
    rows.append(jnp.zeros((8 - N_COND, w.shape[1]), F32))
    o_ref[0] = jnp.concatenate(rows, axis=0) + b_ref[0]


def _adaln(cond_t, ada_w, ada_b):
    depth, d, n = ada_w.shape
    return pl.pallas_call(
        _adaln_kernel,
        grid=(depth, n // ADA_TN),
        in_specs=[
            pl.BlockSpec((d, 8), lambda l, j: (0, 0)),
            pl.BlockSpec((1, d, ADA_TN), lambda l, j: (l, 0, j)),
            pl.BlockSpec((1, 1, ADA_TN), lambda l, j: (l, 0, j)),
        ],
        out_specs=pl.BlockSpec((1, 8, ADA_TN), lambda l, j: (l, 0, j)),
        out_shape=jax.ShapeDtypeStruct((depth, 8, n), F32),
        compiler_params=_params("arbitrary", "arbitrary"),
        name="adaln",
    )(cond_t, ada_w, ada_b.reshape(depth, 1, n))


def _even_in_kernel(x_ref, mod_ref, win_ref, qg_ref, kvg_ref, wq_ref, wqr_ref, wk_ref, wvt_ref,
                    cos_ref, sin_ref, q_ref, k_ref, vt_ref, us_ref):
    m = mod_ref[0]
    u = (x_ref[0] * (1.0 + m[1:2]) + m[0:1]).astype(BF16)
    h = _dot(u, win_ref[...])
    c0 = MLA_Q_LORA
    c1 = c0 + MLA_KV_LORA
    c2 = c1 + S5_WIDTH
    us_ref[0] = h[:, c1:c2].astype(BF16)
    cos = cos_ref[...]
    sin = sin_ref[...]
    kr = h[:, c2:c2 + LANE] * cos + h[:, c2 + LANE:c2 + 2 * LANE] * sin
    cqn = _rms(h[:, :c0], qg_ref[...]).astype(BF16)
    qa = _dot(cqn, wq_ref[...])
    qb = _dot(cqn, wqr_ref[...])
    kvn = _rms(h[:, c0:c1], kvg_ref[...]).astype(BF16)
    ka = _dot(kvn, wk_ref[...])
    vt = _dot_nt(wvt_ref[...], kvn).astype(BF16)
    ones = jnp.ones((V_ROWS - MLA_V, vt.shape[1]), BF16)
    for hh in range(MLA_HEADS):
        vt_ref[0, hh * V_ROWS:hh * V_ROWS + MLA_V] = vt[hh * MLA_V:(hh + 1) * MLA_V]
        vt_ref[0, hh * V_ROWS + MLA_V:(hh + 1) * V_ROWS] = ones
    for hh in range(MLA_HEADS):
        sl = slice(hh * HEAD_PAD, (hh + 1) * HEAD_PAD)
        q_ref[0, :, sl] = ((qa[:, sl] * cos + qb[:, sl] * sin) * Q_SCALE).astype(BF16)
        k_ref[0, :, sl] = (ka[:, sl] + kr).astype(BF16)


def _even_in(x, mod, w, cos, sin, tm):
    b, l, d = x.shape
    hp = MLA_HEADS * HEAD_PAD
    row = lambda n: pl.BlockSpec((1, tm, n), lambda bi, i: (bi, i, 0))
    return pl.pallas_call(
        _even_in_kernel,
        grid=(b, l // tm),
        in_specs=[
            row(d),
            pl.BlockSpec((1, 8, d), lambda bi, i: (bi, 0, 0)),
            _const_spec(w["w_in"].shape), _const_spec(w["q_norm"].shape), _const_spec(w["kv_norm"].shape),
            _const_spec(w["wq"].shape), _const_spec(w["wq_rot"].shape), _const_spec(w["wk"].shape),
            _const_spec(w["wv_t"].shape),
            pl.BlockSpec((tm, HEAD_PAD), lambda bi, i: (i, 0)),
            pl.BlockSpec((tm, HEAD_PAD), lambda bi, i: (i, 0)),
        ],
        out_specs=[row(hp), row(hp),
                   pl.BlockSpec((1, MLA_HEADS * V_ROWS, tm), lambda bi, i: (bi, 0, i)),
                   row(S5_WIDTH)],
        out_shape=[
            jax.ShapeDtypeStruct((b, l, hp), BF16),
            jax.ShapeDtypeStruct((b, l, hp), BF16),
            jax.ShapeDtypeStruct((b, MLA_HEADS * V_ROWS, l), BF16),
            jax.ShapeDtypeStruct((b, l, S5_WIDTH), BF16),
        ],
        compiler_params=_params("parallel", "parallel"),
        name="even_in",
    )(x, mod, w["w_in"], w["q_norm"], w["kv_norm"], w["wq"], w["wq_rot"], w["wk"], w["wv_t"], cos, sin)


ATT_TQ = 512
ATT_TK = 512
ATT_UNROLL = 6


HEADS_PER_STEP = 2
BF16_SUBLANES = 16
V_ROWS = MLA_V + BF16_SUBLANES


def _attn_kernel(q_ref, *refs, kv_lens):
    o_ref, s_buf = refs[-2:]
    kv_refs = refs[:-2]
    tq = q_ref.shape[1]
    tk = s_buf.shape[2]
    heads = range(HEADS_PER_STEP)
    q_t = [q_ref[0, :, hh * HEAD_PAD:(hh + 1) * HEAD_PAD].astype(F32).T.astype(BF16) for hh in heads]

    def fill(slot, src, start, size):
        kc = kv_refs[2 * src][0, pl.ds(start, size), :]
        tops = []
        for hh in heads:
            s = _dot(kc[:, hh * HEAD_PAD:(hh + 1) * HEAD_PAD], q_t[hh])
            s_buf[slot, hh, :size] = s
            tops.append(jnp.max(s, axis=0, keepdims=True))
        return tuple(tops)

    def drain(slot, src, start, size, tops, carry):
        vt = kv_refs[2 * src + 1][0, :, pl.ds(start, size)]
        new = []
        for hh in heads:
            m, acc = carry[hh]
            m_new = jnp.maximum(m, tops[hh])
            p = jnp.exp2(s_buf[slot, hh, :size] - m_new).astype(BF16)
            acc = jnp.exp2(m - m_new) * acc + _dot(vt[hh * V_ROWS:(hh + 1) * V_ROWS], p)
            new.append((m_new, acc))
        return tuple(new)

    carry = tuple((jnp.full((1, tq), -jnp.inf, F32), jnp.zeros((V_ROWS, tq), F32)) for _ in heads)
    chunks = [(src, j0, min(tk, n - j0)) for src, n in enumerate(kv_lens) for j0 in range(0, n, tk)]
    trips = (kv_lens[0] // tk - 1) // ATT_UNROLL if kv_lens[0] % tk == 0 else 0
    trips = trips if trips >= 2 else 0
    tops = fill(0, *chunks[0])
    if trips:
        main = lambda j: (0, pl.multiple_of(j * tk, tk), tk)

        def body(t, state):
            tops, c = state
            for i in range(ATT_UNROLL):
                j = t * ATT_UNROLL + i
                tops_next = fill((i + 1) % 2, *main(j + 1))
                c = drain(i % 2, *main(j), tops, c)
                tops = tops_next
            return tops, c

        tops, carry = lax.fori_loop(0, trips, body, (tops, carry))
    for j in range(trips * ATT_UNROLL, len(chunks)):
        tops_next = fill((j + 1) % 2, *chunks[j + 1]) if j + 1 < len(chunks) else None
        carry = drain(j % 2, *chunks[j], tops, carry)
        tops = tops_next
    out_t = jnp.concatenate([acc[:MLA_V] / acc[MLA_V:MLA_V + 1] for _, acc in carry], axis=0)
    o_ref[0] = out_t.T.astype(BF16)


def _attention(q, kvs):
    b, lq, _ = q.shape
    tq = min(ATT_TQ, lq)
    hs = HEADS_PER_STEP
    in_specs = [pl.BlockSpec((1, tq, hs * HEAD_PAD), lambda bi, hp, i: (bi, i, hp))]
    args = [q]
    for k, vt in kvs:
        n = k.shape[1]
        in_specs.append(pl.BlockSpec((1, n, hs * HEAD_PAD), lambda bi, hp, i: (bi, 0, hp)))
        in_specs.append(pl.BlockSpec((1, hs * V_ROWS, n), lambda bi, hp, i: (bi, hp, 0)))
        args += [k, vt]
    return pl.pallas_call(
        functools.partial(_attn_kernel, kv_lens=tuple(k.shape[1] for k, _ in kvs)),
        grid=(b, MLA_HEADS // hs, lq // tq),
        in_specs=in_specs,
        out_specs=pl.BlockSpec((1, tq, hs * MLA_V), lambda bi, hp, i: (bi, i, hp)),
        out_shape=jax.ShapeDtypeStruct((b, lq, MLA_HEADS * MLA_V), BF16),
        scratch_shapes=[pltpu.VMEM((2, hs, ATT_TK, tq), F32)],
        compiler_params=_params("parallel", "parallel", "parallel"),
        name="attention",
    )(*args)


def _s5_x_kernel(uc_ref, ul_ref, w_ref, xc_ref, xl_ref):
    xc_ref[...] = _dot(uc_ref[0], w_ref[0])
    xl_ref[...] = _dot(ul_ref[0], w_ref[0])


def _s5_chunk_inputs(u_ctx, u_lat, wx):
    g, _, n = u_lat.shape
    rows = lambda u: pl.BlockSpec((1, u.shape[1], n), lambda gi: (gi, 0, 0))
    cols = lambda u: pl.BlockSpec((u.shape[1], n), lambda gi: (0, gi))
    return pl.pallas_call(
        _s5_x_kernel,
        grid=(g,),
        in_specs=[rows(u_ctx), rows(u_lat), pl.BlockSpec((1, n, n), lambda gi: (gi, 0, 0))],
        out_specs=[cols(u_ctx), cols(u_lat)],
        out_shape=[jax.ShapeDtypeStruct((u.shape[1], g * n), F32) for u in (u_ctx, u_lat)],
        compiler_params=_params("parallel"),
        name="s5_chunk_inputs",
    )(u_ctx, u_lat, wx)


S5_SCAN_GROUPS = 16


def _s5_scan_kernel(xc_ref, xl_ref, lre_ref, lim_ref, sc_ref, sl_ref):
    ns = S5_STATE
    lre = lre_ref[...]
    lim = lim_ref[...]
    is_fwd = lax.broadcasted_iota(jnp.int32, lre.shape, 1) < ns

    def walk(x_ref, s_ref, carry):
        n = x_ref.shape[1]

        def step(i, carry):
            sre, sim = carry
            cf = i
            cb = n - 1 - i
            s_ref[0, cf, :, 0:ns] = sre[:, :ns]
            s_ref[0, cb, :, ns:2 * ns] = sre[:, ns:]
            s_ref[0, cf, :, 2 * ns:3 * ns] = sim[:, :ns]
            s_ref[0, cb, :, 3 * ns:] = sim[:, ns:]
            xf = x_ref[0, cf]
            xb = x_ref[0, cb]
            xre = jnp.where(is_fwd, xf[:, :2 * ns], xb[:, :2 * ns])
            xim = jnp.where(is_fwd, xf[:, 2 * ns:], xb[:, 2 * ns:])
            return sre * lre - sim * lim + xre, sre * lim + sim * lre + xim

        return lax.fori_loop(0, n, step, carry)

    zero = jnp.zeros(lre.shape, F32)
    walk(xl_ref, sl_ref, walk(xc_ref, sc_ref, (zero, zero)))


def _s5_scan(x_ctx, x_lat, p1, p2):
    b, _, g, w = x_lat.shape
    gb = S5_SCAN_GROUPS
    chunks = lambda x: pl.BlockSpec((1, x.shape[1], gb, w), lambda bi, gi: (bi, 0, gi, 0))
    pole = pl.BlockSpec((gb, w // 2), lambda bi, gi: (gi, 0))
    return pl.pallas_call(
        _s5_scan_kernel,
        grid=(b, g // gb),
        in_specs=[chunks(x_ctx), chunks(x_lat), pole, pole],
        out_specs=[chunks(x_ctx), chunks(x_lat)],
        out_shape=[jax.ShapeDtypeStruct(x.shape, F32) for x in (x_ctx, x_lat)],
        compiler_params=_params("parallel", "parallel"),
        name="s5_scan",
    )(x_ctx, x_lat, p1, p2)


def _s5_y_kernel(uc_ref, sc_ref, ul_ref, sl_ref, w_ref, yc_ref, yl_ref):
    n = ul_ref.shape[2]
    for u_ref, s_ref, y_ref in ((uc_ref, sc_ref, yc_ref), (ul_ref, sl_ref, yl_ref)):
        y = _dot(u_ref[0], w_ref[0, :n]) + _dot(s_ref[...].astype(BF16), w_ref[0, n:])
        y_ref[0] = y.astype(y_ref.dtype)


def _s5_outputs(u_ctx, s_ctx, u_lat, s_lat, wy):
    g, _, n = u_lat.shape
    rows = lambda u: pl.BlockSpec((1, u.shape[1], n), lambda gi: (gi, 0, 0))
    cols = lambda u: pl.BlockSpec((u.shape[1], n), lambda gi: (0, gi))
    return pl.pallas_call(
        _s5_y_kernel,
        grid=(g,),
        in_specs=[rows(u_ctx), cols(u_ctx), rows(u_lat), cols(u_lat),
                  pl.BlockSpec((1, 2 * n, n), lambda gi: (gi, 0, 0))],
        out_specs=[rows(u_ctx), rows(u_lat)],
        out_shape=[jax.ShapeDtypeStruct(u.shape, BF16) for u in (u_ctx, u_lat)],
        compiler_params=_params("parallel"),
        name="s5_outputs",
    )(u_ctx, s_ctx, u_lat, s_lat, wy)


def _cmul(ar, ai, br, bi):
    return ar * br - ai * bi, ar * bi + ai * br


def _s5_weights(lam_re, lam_im, log_dt, b_re, b_im, c_re, c_im):
    t = S5_CHUNK
    hi = lax.Precision.HIGHEST
    dt = jnp.exp(log_dt)[..., None]
    mag = jnp.exp(lam_re * dt)
    lb_re = mag * jnp.cos(lam_im * dt)
    lb_im = mag * jnp.sin(lam_im * dt)
    den = lam_re * lam_re + lam_im * lam_im
    nr = lb_re - 1.0
    fr = (nr * lam_re + lb_im * lam_im) / den
    fi = (lb_im * lam_re - nr * lam_im) / den
    bb_re = fr[..., None] * b_re - fi[..., None] * b_im
    bb_im = fr[..., None] * b_im + fi[..., None] * b_re
    pw = [(jnp.ones_like(lb_re), jnp.zeros_like(lb_im))]
    for _ in range(t):
        pw.append(_cmul(pw[-1][0], pw[-1][1], lb_re, lb_im))
    pw_re = jnp.stack([p[0] for p in pw])
    pw_im = jnp.stack([p[1] for p in pw])
    pb_re, pb_im = _cmul(pw_re[:t, ..., None], pw_im[:t, ..., None], bb_re[None], bb_im[None])
    kern = (jnp.einsum('dgpn,ldgnq->ldgpq', c_re, pb_re, precision=hi)
            - jnp.einsum('dgpn,ldgnq->ldgpq', c_im, pb_im, precision=hi))
    lag = jnp.arange(t)[None, :] - jnp.arange(t)[:, None]
    pick = lambda d: (d[..., None] == jnp.arange(t)).astype(F32)
    kf = jnp.einsum('stl,lgpq->stgpq', pick(lag), kern[:, 0], precision=hi)
    kb = jnp.einsum('stl,lgpq->stgpq', pick(-lag), kern[:, 1], precision=hi)
    g = lam_re.shape[1]
    n_lane = t * S5_GROUP
    toep = (kf + kb).transpose(2, 0, 4, 1, 3).reshape(g, n_lane, n_lane)

    def to_state(w):
        return w.transpose(1, 0, 3, 2).reshape(g, n_lane, S5_STATE)

    wx = jnp.concatenate([to_state(pb_re[::-1, 0]), to_state(pb_re[:, 1]),
                          to_state(pb_im[::-1, 0]), to_state(pb_im[:, 1])], axis=-1)

    def from_state(d, p_re, p_im):
        cr = c_re[d][None]
        ci = c_im[d][None]
        a = cr * p_re[:, :, None, :] - ci * p_im[:, :, None, :]
        bneg = -(cr * p_im[:, :, None, :] + ci * p_re[:, :, None, :])
        lanes = lambda w: w.transpose(1, 3, 0, 2).reshape(g, S5_STATE, n_lane)
        return lanes(a), lanes(bneg)

    f_re, f_im = from_state(0, pw_re[1:, 0], pw_im[1:, 0])
    b_re, b_im = from_state(1, pw_re[t:0:-1, 1], pw_im[t:0:-1, 1])
    wy = jnp.concatenate([toep, f_re, b_re, f_im, b_im], axis=1)
    lt_re, lt_im = pw_re[t], pw_im[t]
    p1 = jnp.concatenate([lt_re[0], lt_re[1]], axis=-1)
    p2 = jnp.concatenate([lt_im[0], lt_im[1]], axis=-1)
    return wx.astype(BF16), wy.astype(BF16), p1, p2


S5_RELAYOUT_CHUNKS = 64
LANE_GROUPS = LANE // S5_GROUP


def _to_groups_kernel(us_ref, u_ref, tok_ref):
    t, p = S5_CHUNK, S5_GROUP
    nc = u_ref.shape[1]
    tiles = S5_WIDTH // LANE
    for q in range(tiles):
        tok_ref[q] = us_ref[0, :, q * LANE:(q + 1) * LANE].astype(F32)
    for s in range(t):
        si = s % LANE_GROUPS
        for q in range(tiles):
            piece = tok_ref[q, pl.ds(s, nc, stride=t), :]
            for gi in range(LANE_GROUPS):
                shift = ((si - gi) * p) % LANE
                moved = pltpu.roll(piece, shift, 1) if shift else piece
                u_ref[q * LANE_GROUPS + gi, :, s * p:(s + 1) * p] = moved[:, si * p:(si + 1) * p].astype(BF16)


def _from_groups_kernel(y_ref, o_ref, step_ref, tok_ref):
    t, p = S5_CHUNK, S5_GROUP
    nc = y_ref.shape[1]
    for g in range(S5_GROUPS):
        q, gi = divmod(g, LANE_GROUPS)
        for so in range(t * p // LANE):
            piece = y_ref[g, :, so * LANE:(so + 1) * LANE].astype(F32)
            for si in range(LANE_GROUPS):
                shift = ((gi - si) * p) % LANE
                moved = pltpu.roll(piece, shift, 1) if shift else piece
                step_ref[so * LANE_GROUPS + si, q, :, gi * p:(gi + 1) * p] = moved[:, gi * p:(gi + 1) * p]
    for q in range(S5_WIDTH // LANE):
        for s in range(t):
            tok_ref[q, pl.ds(s, nc, stride=t), :] = step_ref[s, q]
        o_ref[0, :, q * LANE:(q + 1) * LANE] = tok_ref[q].astype(o_ref.dtype)


def _relayout_tile(n_chunks):
    return min(S5_RELAYOUT_CHUNKS, n_chunks)


def _s5_to_groups(us):
    b, n, w = us.shape
    t = S5_CHUNK
    nc = _relayout_tile(n // t)
    steps = n // t // nc
    return pl.pallas_call(
        _to_groups_kernel,
        grid=(b, steps),
        in_specs=[pl.BlockSpec((1, nc * t, w), lambda bi, i: (bi, i, 0))],
        out_specs=pl.BlockSpec((S5_GROUPS, nc, t * S5_GROUP), lambda bi, i: (0, bi * steps + i, 0)),
        out_shape=jax.ShapeDtypeStruct((S5_GROUPS, b * (n // t), t * S5_GROUP), BF16),
        scratch_shapes=[pltpu.VMEM((w // LANE, nc * t, LANE), F32)],
        compiler_params=_params("parallel", "parallel"),
        name="s5_to_groups",
    )(us)


def _s5_from_groups(y, b):
    g, r, tp = y.shape
    t = S5_CHUNK
    n = r // b * t
    nc = _relayout_tile(n // t)
    steps = n // t // nc
    w = g * S5_GROUP
    return pl.pallas_call(
        _from_groups_kernel,
        grid=(b, steps),
        in_specs=[pl.BlockSpec((g, nc, tp), lambda bi, i: (0, bi * steps + i, 0))],
        out_specs=pl.BlockSpec((1, nc * t, w), lambda bi, i: (bi, i, 0)),
        out_shape=jax.ShapeDtypeStruct((b, n, w), BF16),
        scratch_shapes=[pltpu.VMEM((t, w // LANE, nc, LANE), F32), pltpu.VMEM((w // LANE, nc * t, LANE), F32)],
        compiler_params=_params("parallel", "parallel"),
        name="s5_from_groups",
    )(y)


def _s5_layer(us_ctx, us_lat, w):
    b = us_ctx.shape[0]
    g = S5_GROUPS
    u_c, u_l = _s5_to_groups(us_ctx), _s5_to_groups(us_lat)
    x_c, x_l = _s5_chunk_inputs(u_c, u_l, w["wx"])
    as_chunks = lambda x: x.reshape(b, x.shape[0] // b, g, 2 * LANE)
    s_c, s_l = _s5_scan(as_chunks(x_c), as_chunks(x_l), w["p1"], w["p2"])
    y_c, y_l = _s5_outputs(u_c, s_c.reshape(x_c.shape), u_l, s_l.reshape(x_l.shape), w["wy"])
    return _s5_from_groups(y_c, b), _s5_from_groups(y_l, b)


FFN_CHUNK = FFN_HIDDEN // 2


def _mid_tail(x, y, m, wffn_ref, lng_ref, lnb_ref, x1_ref, a_ref, gt_ref):
    x1 = _layer_norm(DN_ALPHA * x + m[2:3] * y, lng_ref[...], lnb_ref[...])
    x1_ref[0] = x1
    uf = (x1 * (1.0 + m[4:5]) + m[3:4]).astype(BF16)
    for j in range(FFN_HIDDEN // FFN_CHUNK):
        sl = slice(j * FFN_CHUNK, (j + 1) * FFN_CHUNK)
        a_ref[0, :, sl] = _dot(uf, wffn_ref[:, sl]).astype(BF16)
        gsl = slice(FFN_HIDDEN + j * FFN_CHUNK, FFN_HIDDEN + (j + 1) * FFN_CHUNK)
        gt_ref[0, :, sl] = _dot(uf, wffn_ref[:, gsl]).astype(BF16)


def _gelu_tanh(x):
    return 0.5 * x * (1.0 + jnp.tanh(math.sqrt(2.0 / math.pi) * (x + 0.044715 * x * x * x)))


def _even_mid_kernel(x_ref, mod_ref, att_ref, ys_ref, us_ref, d_ref, wglu_ref, wout_ref, lng_ref, lnb_ref,
                     wffn_ref, x1_ref, a_ref, gt_ref):
    z = _gelu_tanh(ys_ref[0].astype(F32) + d_ref[...] * us_ref[0].astype(F32))
    s5 = z * _sigmoid(_dot(z.astype(BF16), wglu_ref[...]))
    n_att = att_ref.shape[2]
    y = _dot(att_ref[0], wout_ref[:n_att]) + _dot(s5.astype(BF16), wout_ref[n_att:])
    _mid_tail(x_ref[0], y, mod_ref[0], wffn_ref, lng_ref, lnb_ref, x1_ref, a_ref, gt_ref)


def _odd_mid_kernel(x_ref, mod_ref, of_ref, ob_ref, g_ref, hn_ref, wout_ref, lng_ref, lnb_ref,
                    wffn_ref, x1_ref, a_ref, gt_ref):
    o = of_ref[0] + ob_ref[0]
    g = g_ref[0]
    gate = g * _sigmoid(g)
    parts = []
    for hh in range(HG_HEADS):
        sl = slice(hh * HG_DV, (hh + 1) * HG_DV)
        parts.append((_rms(o[:, sl], hn_ref[...]) * gate[:, sl]).astype(BF16))
    y = _dot(jnp.concatenate(parts, axis=-1), wout_ref[...])
    _mid_tail(x_ref[0], y, mod_ref[0], wffn_ref, lng_ref, lnb_ref, x1_ref, a_ref, gt_ref)


MID_TM = 512


def _mid_call(kernel, name, x, mod, row_args, row_specs, const_args):
    b, l, d = x.shape
    tm = min(MID_TM, l)
    row = lambda n: pl.BlockSpec((1, tm, n), lambda bi, i: (bi, i, 0))
    in_specs = [row(d), pl.BlockSpec((1, 8, d), lambda bi, i: (bi, 0, 0))]
    in_specs += [spec(tm) for spec in row_specs]
    in_specs += [_const_spec(a.shape) for a in const_args]
    return pl.pallas_call(
        kernel,
        grid=(b, l // tm),
        in_specs=in_specs,
        out_specs=[row(d), row(FFN_HIDDEN), row(FFN_HIDDEN)],
        out_shape=[jax.ShapeDtypeStruct((b, l, d), F32),
                   jax.ShapeDtypeStruct((b, l, FFN_HIDDEN), BF16),
                   jax.ShapeDtypeStruct((b, l, FFN_HIDDEN), BF16)],
        compiler_params=_params("parallel", "parallel"),
        name=name,
    )(x, mod, *row_args, *const_args)


def _row_spec(n, lane_block=0):
    return lambda tm: pl.BlockSpec((1, tm, n), lambda bi, i: (bi, i, lane_block))


def _even_mid(x, mod, att, ys, us, w):
    return _mid_call(
        _even_mid_kernel, "even_mid", x, mod, [att, ys, us],
        [_row_spec(att.shape[2]), _row_spec(S5_WIDTH), _row_spec(S5_WIDTH)],
        [w["s5_d"], w["w_glu"], w["w_out"], w["ln_g0"], w["ln_b0"], w["ffn_w_in"]])


def _odd_mid(x, mod, o_f, o_b, hg, w):
    return _mid_call(
        _odd_mid_kernel, "odd_mid", x, mod, [o_f, o_b, hg],
        [_row_spec(HG_WIDTH), _row_spec(HG_WIDTH), _row_spec(HG_WIDTH, 4)],
        [w["hg_norm"], w["w_out"], w["ln_g0"], w["ln_b0"], w["ffn_w_in"]])


FFN_TM = 256
HALO = 8


def _ffn_out_kernel(a_ref, ap_ref, an_ref, gt_ref, x1_ref, mod_ref, cw_ref, cb_ref, wout_ref, lng_ref, lnb_ref,
                    o_ref):
    i = pl.program_id(1)
    tm = a_ref.shape[1]
    a = a_ref[0].astype(F32)
    row = lax.broadcasted_iota(jnp.int32, (HALO, 1), 0)
    before = jnp.where(i == 0, 0.0, ap_ref[0, HALO - 1:HALO, :].astype(F32))
    after = jnp.where(i == pl.num_programs(1) - 1, 0.0, an_ref[0, 0:1, :].astype(F32))
    a_prev = pltpu.roll(a, 1, 0)
    a_prev = jnp.concatenate([jnp.where(row == 0, before, a_prev[:HALO]), a_prev[HALO:]], axis=0)
    a_next = pltpu.roll(a, tm - 1, 0)
    a_next = jnp.concatenate([a_next[:tm - HALO], jnp.where(row == HALO - 1, after, a_next[tm - HALO:])], axis=0)
    cw = cw_ref[...]
    conv = cb_ref[...] + cw[0:1] * a_prev + cw[1:2] * a + cw[2:3] * a_next
    hidden = (_silu(conv) * gt_ref[0].astype(F32)).astype(BF16)
    f = _dot(hidden, wout_ref[...])
    m = mod_ref[0]
    o_ref[0] = _layer_norm(DN_ALPHA * x1_ref[0] + m[5:6] * f, lng_ref[...], lnb_ref[...])


def _ffn_out(a, gt, x1, mod, w):
    b, l, d = x1.shape
    tm = min(FFN_TM, l)
    per = tm // HALO
    last = l // HALO - 1
    row = lambda n: pl.BlockSpec((1, tm, n), lambda bi, i: (bi, i, 0))
    consts = [w["conv_w"], w["conv_b"], w["ffn_w_out"], w["ln_g1"], w["ln_b1"]]
    return pl.pallas_call(
        _ffn_out_kernel,
        grid=(b, l // tm),
        in_specs=[
            row(FFN_HIDDEN),
            pl.BlockSpec((1, HALO, FFN_HIDDEN), lambda bi, i: (bi, jnp.maximum(i * per - 1, 0), 0)),
            pl.BlockSpec((1, HALO, FFN_HIDDEN), lambda bi, i: (bi, jnp.minimum((i + 1) * per, last), 0)),
            row(FFN_HIDDEN), row(d),
            pl.BlockSpec((1, 8, d), lambda bi, i: (bi, 0, 0)),
        ] + [_const_spec(c.shape) for c in consts],
        out_specs=row(d),
        out_shape=jax.ShapeDtypeStruct((b, l, d), F32),
        compiler_params=_params("parallel", "parallel"),
        name="ffn_out",
    )(a, a, a, gt, x1, mod, *consts)


ODD_TM = 256


def _odd_in_kernel(x_ref, mod_ref, w_ref, o_ref):
    m = mod_ref[0]
    u = (x_ref[0] * (1.0 + m[1:2]) + m[0:1]).astype(BF16)
    for j in range(HG_IN // HG_WIDTH):
        sl = slice(j * HG_WIDTH, (j + 1) * HG_WIDTH)
        o_ref[0, :, sl] = _dot(u, w_ref[:, sl])


def _odd_in(x, mod, w_in):
    b, l, d = x.shape
    tm = min(ODD_TM, l)
    return pl.pallas_call(
        _odd_in_kernel,
        grid=(b, l // tm),
        in_specs=[pl.BlockSpec((1, tm, d), lambda bi, i: (bi, i, 0)),
                  pl.BlockSpec((1, 8, d), lambda bi, i: (bi, 0, 0)),
                  _const_spec(w_in.shape)],
        out_specs=pl.BlockSpec((1, tm, HG_IN), lambda bi, i: (bi, i, 0)),
        out_shape=jax.ShapeDtypeStruct((b, l, HG_IN), F32),
        compiler_params=_params("parallel", "parallel"),
        name="odd_in",
    )(x, mod, w_in)


def _hgrn_direction(q, pre, v, lb, state, level, reverse):
    c = q.shape[0]
    row = lax.broadcasted_iota(jnp.int32, q.shape, 0)
    pos = (c - 1 - row) if reverse else row

    def prev(x, k):
        return pltpu.roll(x, (c - k) if reverse else k, 0)

    def nxt(x, k):
        return pltpu.roll(x, k if reverse else (c - k), 0)

    f = lb + (1.0 - lb) * _sigmoid(pre)
    key = 1.0 - f
    cum = jnp.log(f)
    k = 1
    while k < c:
        cum = cum + jnp.where(pos >= k, prev(cum, k), 0.0)
        k *= 2
    last = cum[0:1] if reverse else cum[c - 1:c]

    qb = q.astype(BF16)
    kb = key.astype(BF16)
    scores = jnp.where(level == 0, _dot_nt(qb, kb), 0.0)
    z = cum
    half = 1
    lvl = 1
    while half < c:
        ref = jnp.where((pos & half) == 0, z, prev(z, half))
        e = jnp.exp(-jnp.abs(cum - ref))
        x = (jnp.where((pos & half) != 0, q, key) * e).astype(BF16)
        scores = jnp.where(level == lvl, _dot_nt(x, x), scores)
        z = jnp.where((pos & half) != 0, z, nxt(z, half))
        half *= 2
        lvl += 1

    o = _dot(scores.astype(BF16), v)
    o = o + _dot_nt((q * jnp.exp(cum)).astype(BF16), state.astype(BF16))
    kd = (key * jnp.exp(last - cum)).astype(BF16)
    new_state = jnp.exp(last) * state + _dot_tn(v, kd)
    return o, new_state


def _hgrn_kernel(qf_ref, pf_ref, vf_ref, qb_ref, pb_ref, vb_ref, lb_ref, lvf_ref, lvb_ref, s0_ref,
                 of_ref, ob_ref, sfin_ref, st_ref):
    ci = pl.program_id(2)

    @pl.when(ci == 0)
    def _():
        st_ref[...] = s0_ref[0, 0]

    o, s = _hgrn_direction(qf_ref[0], pf_ref[0], vf_ref[0].astype(BF16), lb_ref[0], st_ref[0],
                           lvf_ref[...], False)
    of_ref[0] = o
    st_ref[0] = s
    o, s = _hgrn_direction(qb_ref[0], pb_ref[0], vb_ref[0].astype(BF16), lb_ref[1], st_ref[1],
                           lvb_ref[...], True)
    ob_ref[0] = o
    st_ref[1] = s

    @pl.when(ci == pl.num_programs(2) - 1)
    def _():
        sfin_ref[0, 0] = st_ref[...]


def _hgrn_levels(c):
    p = jnp.arange(c, dtype=jnp.int32)
    x = p[:, None] ^ p[None, :]
    lvl = jnp.where(x == 0, 0, 32 - lax.clz(x))
    fwd = jnp.where(p[:, None] >= p[None, :], lvl, -1)
    bwd = jnp.where(p[:, None] <= p[None, :], lvl, -1)
    return fwd.astype(jnp.int32), bwd.astype(jnp.int32)


def _hgrn(hg, lb, s0):
    b, l, _ = hg.shape
    c = HG_CHUNK
    n = l // c
    nh = HG_HEADS
    lvf, lvb = _hgrn_levels(c)
    fw = lambda blk: pl.BlockSpec((1, c, HG_DK), lambda bi, h, ci: (bi, ci, blk * nh + h))
    bw = lambda blk: pl.BlockSpec((1, c, HG_DK), lambda bi, h, ci: (bi, n - 1 - ci, blk * nh + h))
    state_spec = pl.BlockSpec((1, 1, 2, HG_DV, HG_DK), lambda bi, h, ci: (bi, h, 0, 0, 0))
    return pl.pallas_call(
        _hgrn_kernel,
        grid=(b, nh, n),
        in_specs=[fw(0), fw(1), fw(3), bw(0), bw(2), bw(3),
                  pl.BlockSpec((2, 1, HG_DK), lambda bi, h, ci: (0, 0, h)),
                  pl.BlockSpec((c, c), lambda bi, h, ci: (0, 0)),
                  pl.BlockSpec((c, c), lambda bi, h, ci: (0, 0)),
                  state_spec],
        out_specs=[pl.BlockSpec((1, c, HG_DV), lambda bi, h, ci: (bi, ci, h)),
                   pl.BlockSpec((1, c, HG_DV), lambda bi, h, ci: (bi, n - 1 - ci, h)),
                   state_spec],
        out_shape=[jax.ShapeDtypeStruct((b, l, HG_WIDTH), F32),
                   jax.ShapeDtypeStruct((b, l, HG_WIDTH), F32),
                   jax.ShapeDtypeStruct((b, nh, 2, HG_DV, HG_DK), F32)],
        scratch_shapes=[pltpu.VMEM((2, HG_DV, HG_DK), F32)],
        compiler_params=_params("parallel", "parallel", "arbitrary"),
        name="hgrn",
    )(hg, hg, hg, hg, hg, hg, lb, lvf, lvb, s0)


HG_DIRECT_CHUNK = 64
HG_DIRECT_BLOCK = 1024
HG_DIRECT_MAX_EXPONENT = 60.0
HG_DIRECT_MIN_LB = math.exp(-HG_DIRECT_MAX_EXPONENT / (HG_DIRECT_CHUNK // 2))


def _split3(x):
    hi = x.astype(BF16)
    r = x - hi.astype(F32)
    mid = r.astype(BF16)
    lo = (r - mid.astype(F32)).astype(BF16)
    return hi, mid, lo


def _hgrn_direct_block(q_ref, p_ref, v_ref, o_ref, lb, state, tri3, visible, reverse):
    c = HG_DIRECT_CHUNK
    n_sub = q_ref.shape[1] // c
    half = c // 2
    order = list(range(n_sub - 1, -1, -1) if reverse else range(n_sub))
    rows = [slice(j * c, (j + 1) * c) for j in order]
    fs = [lb + (1.0 - lb) * _sigmoid(p_ref[0, r, :]) for r in rows]
    cums = [_dot(tri3, jnp.concatenate(_split3(jnp.log(f)), axis=0)) for f in fs]
    mids = [cum[half:half + 1] if reverse else cum[half - 1:half] for cum in cums]
    lasts = [cum[0:1] if reverse else cum[c - 1:c] for cum in cums]
    qts = [(q_ref[0, r, :] * jnp.exp(cum - mid)).astype(BF16) for r, cum, mid in zip(rows, cums, mids)]
    kts = [((1.0 - f) * jnp.exp(mid - cum)).astype(BF16) for f, cum, mid in zip(fs, cums, mids)]
    vs = [v_ref[0, r, :].astype(BF16) for r in rows]
    scores = [jnp.where(visible, _dot_nt(qt, kt), 0.0).astype(BF16) for qt, kt in zip(qts, kts)]
    adds = [jnp.exp(last - mid) * _dot_tn(v, kt) for last, mid, v, kt in zip(lasts, mids, vs, kts)]
    intra = [_dot(sc, v) for sc, v in zip(scores, vs)]
    entering = []
    for mid, last, add in zip(mids, lasts, adds):
        entering.append((state * jnp.exp(mid)).astype(BF16))
        state = jnp.exp(last) * state + add
    for r, qt, o, s_in in zip(rows, qts, intra, entering):
        o_ref[0, r, :] = o + _dot_nt(qt, s_in)
    return state


def _hgrn_direct_kernel(qf_ref, pf_ref, vf_ref, qb_ref, pb_ref, vb_ref, lb_ref, trif_ref, trib_ref, s0_ref,
                        of_ref, ob_ref, sfin_ref, st_ref):
    ci = pl.program_id(2)
    c = HG_DIRECT_CHUNK

    @pl.when(ci == 0)
    def _():
        st_ref[...] = s0_ref[0, 0]

    row = lax.broadcasted_iota(jnp.int32, (c, c), 0)
    col = lax.broadcasted_iota(jnp.int32, (c, c), 1)
    st_ref[0] = _hgrn_direct_block(qf_ref, pf_ref, vf_ref, of_ref, lb_ref[0], st_ref[0], trif_ref[...],
                                   row >= col, False)
    st_ref[1] = _hgrn_direct_block(qb_ref, pb_ref, vb_ref, ob_ref, lb_ref[1], st_ref[1], trib_ref[...],
                                   row <= col, True)

    @pl.when(ci == pl.num_programs(2) - 1)
    def _():
        sfin_ref[0, 0] = st_ref[...]


def _hgrn_direct(hg, lb, s0):
    b, l, _ = hg.shape
    c = HG_DIRECT_CHUNK
    blk_rows = min(HG_DIRECT_BLOCK, l)
    n = l // blk_rows
    nh = HG_HEADS
    p = jnp.arange(c)
    lower_tri = (p[:, None] >= p[None, :]).astype(BF16)
    trif = jnp.concatenate([lower_tri] * 3, axis=1)
    trib = jnp.concatenate([lower_tri.T] * 3, axis=1)
    fw = lambda blk: pl.BlockSpec((1, blk_rows, HG_DK), lambda bi, h, ci: (bi, ci, blk * nh + h))
    bw = lambda blk: pl.BlockSpec((1, blk_rows, HG_DK), lambda bi, h, ci: (bi, n - 1 - ci, blk * nh + h))
    state_spec = pl.BlockSpec((1, 1, 2, HG_DV, HG_DK), lambda bi, h, ci: (bi, h, 0, 0, 0))
    return pl.pallas_call(
        _hgrn_direct_kernel,
        grid=(b, nh, n),
        in_specs=[fw(0), fw(1), fw(3), bw(0), bw(2), bw(3),
                  pl.BlockSpec((2, 1, HG_DK), lambda bi, h, ci: (0, 0, h)),
                  pl.BlockSpec((c, 3 * c), lambda bi, h, ci: (0, 0)),
                  pl.BlockSpec((c, 3 * c), lambda bi, h, ci: (0, 0)),
                  state_spec],
        out_specs=[pl.BlockSpec((1, blk_rows, HG_DV), lambda bi, h, ci: (bi, ci, h)),
                   pl.BlockSpec((1, blk_rows, HG_DV), lambda bi, h, ci: (bi, n - 1 - ci, h)),
                   state_spec],
        out_shape=[jax.ShapeDtypeStruct((b, l, HG_WIDTH), F32),
                   jax.ShapeDtypeStruct((b, l, HG_WIDTH), F32),
                   jax.ShapeDtypeStruct((b, nh, 2, HG_DV, HG_DK), F32)],
        scratch_shapes=[pltpu.VMEM((2, HG_DV, HG_DK), F32)],
        compiler_params=_params("parallel", "parallel", "arbitrary"),
        name="hgrn_direct",
    )(hg, hg, hg, hg, hg, hg, lb, trif, trib, s0)


def _hgrn_auto(hg, lb, s0):
    return lax.cond(jnp.min(lb) >= HG_DIRECT_MIN_LB,
                    lambda: tuple(_hgrn_direct(hg, lb, s0)), lambda: tuple(_hgrn(hg, lb, s0)))


def _rope_tables(length):
    rows = length // GRID_W
    row = np.repeat(np.arange(rows, dtype=np.float64), GRID_W)
    col = np.tile(np.arange(GRID_W, dtype=np.float64), rows)
    n_freq = MLA_ROPE // 4
    inv = ROPE_BASE ** (-np.arange(n_freq, dtype=np.float64) / n_freq)
    ar = row[:, None] * inv
    ac = col[:, None] * inv
    ang = np.concatenate([ar, ar, ac, ac], axis=-1)
    return np.cos(ang), np.sin(ang)


def _head_tables(cos, sin):
    n = cos.shape[0]
    pad = np.zeros((n, HEAD_PAD - MLA_NOPE - MLA_ROPE))
    return (jnp.asarray(np.concatenate([np.ones((n, MLA_NOPE)), cos, pad], -1), F32),
            jnp.asarray(np.concatenate([np.zeros((n, MLA_NOPE)), sin, pad], -1), F32))


def _rotate_cols(w):
    ws = w.reshape(w.shape[:-1] + (2, 2, MLA_ROPE // 4))
    return jnp.stack([-ws[..., 1, :], ws[..., 0, :]], axis=-2).reshape(w.shape)


def _even_weights(ev_w_in, q_norm, w_uq, kv_norm, w_ukv):
    c0 = MLA_Q_LORA
    c1 = c0 + MLA_KV_LORA
    c2 = c1 + MLA_ROPE
    d = ev_w_in.shape[0]
    w_kr = ev_w_in[:, c1:c2]
    lo = jnp.zeros((d, MLA_NOPE), F32)
    hi = jnp.zeros((d, HEAD_PAD - MLA_NOPE - MLA_ROPE), F32)
    w_in = jnp.concatenate([ev_w_in[:, :c1], ev_w_in[:, c2:], lo, w_kr, hi, lo, _rotate_cols(w_kr), hi], axis=-1)
    wq = w_uq.reshape(c0, MLA_HEADS, MLA_NOPE + MLA_ROPE)
    zq = jnp.zeros((c0, MLA_HEADS, HEAD_PAD - MLA_NOPE - MLA_ROPE), F32)
    wq_pad = jnp.concatenate([wq, zq], -1).reshape(c0, MLA_HEADS * HEAD_PAD)
    wq_rot = jnp.concatenate([jnp.zeros((c0, MLA_HEADS, MLA_NOPE), F32), _rotate_cols(wq[..., MLA_NOPE:]), zq],
                             -1).reshape(c0, MLA_HEADS * HEAD_PAD)
    wkv = w_ukv.reshape(MLA_KV_LORA, MLA_HEADS, MLA_NOPE + MLA_V)
    wk = jnp.concatenate([wkv[..., :MLA_NOPE], jnp.zeros((MLA_KV_LORA, MLA_HEADS, HEAD_PAD - MLA_NOPE), F32)],
                         -1).reshape(MLA_KV_LORA, MLA_HEADS * HEAD_PAD)
    wv_t = wkv[..., MLA_NOPE:].reshape(MLA_KV_LORA, MLA_HEADS * MLA_V).T
    return {"w_in": w_in.astype(BF16), "q_norm": q_norm.reshape(1, -1), "kv_norm": kv_norm.reshape(1, -1),
            "wq": wq_pad.astype(BF16), "wq_rot": wq_rot.astype(BF16), "wk": wk.astype(BF16),
            "wv_t": wv_t.astype(BF16)}


def _layer_mod(mod, layer, b):
    d = D_MODEL
    rows = mod[layer, :, :6 * d].reshape(8, 6, d)
    pad = jnp.zeros((2, d), F32)
    lat = jnp.stack([jnp.concatenate([rows[bi], pad], 0) for bi in range(b)])
    ctx = jnp.broadcast_to(jnp.concatenate([rows[b], pad], 0)[None], (b, 8, d))
    return lat, ctx


def kernel(x, c, ctx, c_ctx, ada_w, ada_b, ln_g, ln_b, ffn_w_in, ffn_conv_w, ffn_conv_b, ffn_w_out, ev_w_in,
           mla_q_norm, mla_w_uq, mla_kv_norm, mla_w_ukv, s5_lam_re, s5_lam_im, s5_log_dt, s5_b_re, s5_b_im,
           s5_c_re, s5_c_im, s5_d, s5_w_glu, ev_w_out, hg_w_in, hg_lb, hg_norm, hg_w_out):
    b, l, d = x.shape
    lc = ctx.shape[1]
    assert b + 1 == N_COND and d == D_MODEL and l % 512 == 0 and lc % 256 == 0

    cond_t = jnp.concatenate([c, c_ctx[None], jnp.zeros((8 - N_COND, d), F32)], 0).T
    mod = _adaln(cond_t, ada_w, ada_b)

    def ffn_weights(layer):
        return {"ln_g0": ln_g[layer, 0].reshape(1, d), "ln_b0": ln_b[layer, 0].reshape(1, d),
                "ln_g1": ln_g[layer, 1].reshape(1, d), "ln_b1": ln_b[layer, 1].reshape(1, d),
                "ffn_w_in": ffn_w_in[layer].astype(BF16),
                "conv_w": jnp.concatenate([ffn_conv_w[layer], jnp.zeros((5, FFN_HIDDEN), F32)], 0),
                "conv_b": ffn_conv_b[layer].reshape(1, FFN_HIDDEN),
                "ffn_w_out": ffn_w_out[layer].astype(BF16)}

    mod_lat, mod_ctx = _layer_mod(mod, 0, b)
    we = _even_weights(ev_w_in[0], mla_q_norm[0], mla_w_uq[0], mla_kv_norm[0], mla_w_ukv[0])
    cos_l, sin_l = _head_tables(*_rope_tables(l))
    cos_c, sin_c = _head_tables(np.ones((lc, MLA_ROPE)), np.zeros((lc, MLA_ROPE)))
    q_l, k_l, v_l, us_l = _even_in(x, mod_lat, we, cos_l, sin_l, min(512, l))
    q_c, k_c, v_c, us_c = _even_in(ctx, mod_ctx, we, cos_c, sin_c, min(512, lc))
    att_l = _attention(q_l, [(k_l, v_l), (k_c, v_c)])
    att_c = _attention(q_c, [(k_c, v_c)])
    wx, wy, p1, p2 = _s5_weights(s5_lam_re[0], s5_lam_im[0], s5_log_dt[0], s5_b_re[0], s5_b_im[0],
                                 s5_c_re[0], s5_c_im[0])
    ys_c, ys_l = _s5_layer(us_c, us_l, {"wx": wx, "wy": wy, "p1": p1, "p2": p2})
    w0 = ffn_weights(0)
    w0.update({"s5_d": s5_d[0].reshape(1, S5_WIDTH), "w_glu": s5_w_glu[0].astype(BF16),
               "w_out": ev_w_out[0].astype(BF16)})
    x1, a, gt = _even_mid(x, mod_lat, att_l, ys_l, us_l, w0)
    x = _ffn_out(a, gt, x1, mod_lat, w0)
    c1, a, gt = _even_mid(ctx, mod_ctx, att_c, ys_c, us_c, w0)
    ctx = _ffn_out(a, gt, c1, mod_ctx, w0)

    mod_lat, mod_ctx = _layer_mod(mod, 1, b)
    sm = jax.nn.softmax(hg_lb, axis=0)
    lower = (jnp.cumsum(sm, axis=0) - sm[0])[1].reshape(2, 1, HG_WIDTH)
    w_hg = hg_w_in[0].astype(BF16)
    hg_c = _odd_in(ctx, mod_ctx, w_hg)
    hg_l = _odd_in(x, mod_lat, w_hg)
    zero_state = jnp.zeros((b, HG_HEADS, 2, HG_DV, HG_DK), F32)
    _, _, s_ctx = _hgrn_auto(hg_c, lower, zero_state)
    o_f, o_b, _ = _hgrn_auto(hg_l, lower, s_ctx)
    w1 = ffn_weights(1)
    w1.update({"hg_norm": hg_norm[0].reshape(1, HG_DV), "w_out": hg_w_out[0].astype(BF16)})
    x1, a, gt = _odd_mid(x, mod_lat, o_f, o_b, hg_l, w1)
    return _ffn_out(a, gt, x1, mod_lat, w1)
```

```python
import functools
import math

import jax
import jax.numpy as jnp
import numpy as np
from jax import lax
from jax.experimental import pallas as pl
from jax.experimental.pallas import tpu as pltpu

F32 = jnp.float32
BF16 = jnp.bfloat16

D_MODEL = 1024
DEPTH = 2
GRID_W = 64
NORM_EPS = 1e-6
DN_ALPHA = (2.0 * DEPTH) ** 0.25

MLA_HEADS = 8
MLA_NOPE = 64
MLA_ROPE = 32
MLA_V = 64
MLA_Q_LORA = 384
MLA_KV_LORA = 256
MLA_SCALE = (MLA_NOPE + MLA_ROPE) ** -0.5
Q_SCALE = MLA_SCALE * math.log2(math.e)
ROPE_BASE = 10000.0

S5_WIDTH = 512
S5_GROUP = 16
S5_GROUPS = S5_WIDTH // S5_GROUP
S5_STATE = 64
S5_CHUNK = 16

HG_HEADS = 8
HG_DK = 128
HG_DV = 128
HG_WIDTH = HG_HEADS * HG_DK
HG_IN = 5 * HG_WIDTH
HG_CHUNK = 128

FFN_HIDDEN = 2816

LANE = 128
HEAD_PAD = 128
EVEN_IN_PAD = MLA_Q_LORA + MLA_KV_LORA + S5_WIDTH + 2 * LANE

VMEM_LIMIT = 56 * 1024 * 1024


def _params(*sem):
    return pltpu.CompilerParams(dimension_semantics=sem, vmem_limit_bytes=VMEM_LIMIT)


def _const_spec(shape):
    zeros = (0,) * len(shape)
    return pl.BlockSpec(shape, lambda *_: zeros, pipeline_mode=pl.Buffered(1))


def _dot(a, b):
    return jnp.dot(a, b, preferred_element_type=F32)


def _dot_nt(a, b):
    return lax.dot_general(a, b, (((1,), (1,)), ((), ())), preferred_element_type=F32)


def _dot_tn(a, b):
    return lax.dot_general(a, b, (((0,), (0,)), ((), ())), preferred_element_type=F32)


def _sigmoid(x):
    return 1.0 / (1.0 + jnp.exp2(x * -math.log2(math.e)))


def _silu(x):
    return x * _sigmoid(x)


def _rms(x, g):
    return x * lax.rsqrt(jnp.mean(x * x, -1, keepdims=True) + NORM_EPS) * g


def _layer_norm(x, g, b):
    mu = jnp.mean(x, -1, keepdims=True)
    xc = x - mu
    var = jnp.mean(xc * xc, -1, keepdims=True)
    return xc * lax.rsqrt(var + NORM_EPS) * g + b


ADA_TN = 512
N_COND = 3


def _adaln_kernel(ct_ref, w_ref, b_ref, o_ref):
    s = _silu(ct_ref[...])
    w = w_ref[0]
    rows = [jnp.sum(s[:, r:r + 1] * w, axis=0, keepdims=True) for r in range(N_COND)]
    rows.append(jnp.zeros((8 - N_COND, w.shape[1]), F32))
    o_ref[0] = jnp.concatenate(rows, axis=0) + b_ref[0]


def _adaln(cond_t, ada_w, ada_b):
    depth, d, n = ada_w.shape
    return pl.pallas_call(
        _adaln_kernel,
        grid=(depth, n // ADA_TN),
        in_specs=[
            pl.BlockSpec((d, 8), lambda l, j: (0, 0)),
            pl.BlockSpec((1, d, ADA_TN), lambda l, j: (l, 0, j)),
            pl.BlockSpec((1, 1, ADA_TN), lambda l, j: (l, 0, j)),
        ],
        out_specs=pl.BlockSpec((1, 8, ADA_TN), lambda l, j: (l, 0, j)),
        out_shape=jax.ShapeDtypeStruct((depth, 8, n), F32),
        compiler_params=_params("arbitrary", "arbitrary"),
        name="adaln",
    )(cond_t, ada_w, ada_b.reshape(depth, 1, n))


def _even_in_kernel(x_ref, mod_ref, win_ref, qg_ref, kvg_ref, wq_ref, wqr_ref, wk_ref, wvt_ref,
                    cos_ref, sin_ref, q_ref, k_ref, vt_ref, us_ref):
    m = mod_ref[0]
    u = (x_ref[0] * (1.0 + m[1:2]) + m[0:1]).astype(BF16)
    h = _dot(u, win_ref[...])
    c0 = MLA_Q_LORA
    c1 = c0 + MLA_KV_LORA
    c2 = c1 + S5_WIDTH
    us_ref[0] = h[:, c1:c2].astype(BF16)
    cos = cos_ref[...]
    sin = sin_ref[...]
    kr = h[:, c2:c2 + LANE] * cos + h[:, c2 + LANE:c2 + 2 * LANE] * sin
    cqn = _rms(h[:, :c0], qg_ref[...]).astype(BF16)
    qa = _dot(cqn, wq_ref[...])
    qb = _dot(cqn, wqr_ref[...])
    kvn = _rms(h[:, c0:c1], kvg_ref[...]).astype(BF16)
    ka = _dot(kvn, wk_ref[...])
    vt = _dot_nt(wvt_ref[...], kvn).astype(BF16)
    ones = jnp.ones((V_ROWS - MLA_V, vt.shape[1]), BF16)
    for hh in range(MLA_HEADS):
        vt_ref[0, hh * V_ROWS:hh * V_ROWS + MLA_V] = vt[hh * MLA_V:(hh + 1) * MLA_V]
        vt_ref[0, hh * V_ROWS + MLA_V:(hh + 1) * V_ROWS] = ones
    for hh in range(MLA_HEADS):
        sl = slice(hh * HEAD_PAD, (hh + 1) * HEAD_PAD)
        q_ref[0, :, sl] = ((qa[:, sl] * cos + qb[:, sl] * sin) * Q_SCALE).astype(BF16)
        k_ref[0, :, sl] = (ka[:, sl] + kr).astype(BF16)


def _even_in(x, mod, w, cos, sin, tm):
    b, l, d = x.shape
    hp = MLA_HEADS * HEAD_PAD
    row = lambda n: pl.BlockSpec((1, tm, n), lambda bi, i: (bi, i, 0))
    return pl.pallas_call(
        _even_in_kernel,
        grid=(b, l // tm),
        in_specs=[
            row(d),
            pl.BlockSpec((1, 8, d), lambda bi, i: (bi, 0, 0)),
            _const_spec(w["w_in"].shape), _const_spec(w["q_norm"].shape), _const_spec(w["kv_norm"].shape),
            _const_spec(w["wq"].shape), _const_spec(w["wq_rot"].shape), _const_spec(w["wk"].shape),
            _const_spec(w["wv_t"].shape),
            pl.BlockSpec((tm, HEAD_PAD), lambda bi, i: (i, 0)),
            pl.BlockSpec((tm, HEAD_PAD), lambda bi, i: (i, 0)),
        ],
        out_specs=[row(hp), row(hp),
                   pl.BlockSpec((1, MLA_HEADS * V_ROWS, tm), lambda bi, i: (bi, 0, i)),
                   row(S5_WIDTH)],
        out_shape=[
            jax.ShapeDtypeStruct((b, l, hp), BF16),
            jax.ShapeDtypeStruct((b, l, hp), BF16),
            jax.ShapeDtypeStruct((b, MLA_HEADS * V_ROWS, l), BF16),
            jax.ShapeDtypeStruct((b, l, S5_WIDTH), BF16),
        ],
        compiler_params=_params("parallel", "parallel"),
        name="even_in",
    )(x, mod, w["w_in"], w["q_norm"], w["kv_norm"], w["wq"], w["wq_rot"], w["wk"], w["wv_t"], cos, sin)


ATT_TQ = 512
ATT_TK = 512
ATT_UNROLL = 6


HEADS_PER_STEP = 2
BF16_SUBLANES = 16
V_ROWS = MLA_V + BF16_SUBLANES


def _attn_kernel(q_ref, *refs, kv_lens):
    o_ref, s_buf = refs[-2:]
    kv_refs = refs[:-2]
    tq = q_ref.shape[1]
    tk = s_buf.shape[2]
    heads = range(HEADS_PER_STEP)
    q_t = [q_ref[0, :, hh * HEAD_PAD:(hh + 1) * HEAD_PAD].astype(F32).T.astype(BF16) for hh in heads]

    def fill(slot, src, start, size):
        kc = kv_refs[2 * src][0, pl.ds(start, size), :]
        tops = []
        for hh in heads:
            s = _dot(kc[:, hh * HEAD_PAD:(hh + 1) * HEAD_PAD], q_t[hh])
            s_buf[slot, hh, :size] = s
            tops.append(jnp.max(s, axis=0, keepdims=True))
        return tuple(tops)

    def drain(slot, src, start, size, tops, carry):
        vt = kv_refs[2 * src + 1][0, :, pl.ds(start, size)]
        new = []
        for hh in heads:
            m, acc = carry[hh]
            m_new = jnp.maximum(m, tops[hh])
            p = jnp.exp2(s_buf[slot, hh, :size] - m_new).astype(BF16)
            acc = jnp.exp2(m - m_new) * acc + _dot(vt[hh * V_ROWS:(hh + 1) * V_ROWS], p)
            new.append((m_new, acc))
        return tuple(new)

    carry = tuple((jnp.full((1, tq), -jnp.inf, F32), jnp.zeros((V_ROWS, tq), F32)) for _ in heads)
    chunks = [(src, j0, min(tk, n - j0)) for src, n in enumerate(kv_lens) for j0 in range(0, n, tk)]
    trips = (kv_lens[0] // tk - 1) // ATT_UNROLL if kv_lens[0] % tk == 0 else 0
    trips = trips if trips >= 2 else 0
    tops = fill(0, *chunks[0])
    if trips:
        main = lambda j: (0, pl.multiple_of(j * tk, tk), tk)

        def body(t, state):
            tops, c = state
            for i in range(ATT_UNROLL):
                j = t * ATT_UNROLL + i
                tops_next = fill((i + 1) % 2, *main(j + 1))
                c = drain(i % 2, *main(j), tops, c)
                tops = tops_next
            return tops, c

        tops, carry = lax.fori_loop(0, trips, body, (tops, carry))
    for j in range(trips * ATT_UNROLL, len(chunks)):
        tops_next = fill((j + 1) % 2, *chunks[j + 1]) if j + 1 < len(chunks) else None
        carry = drain(j % 2, *chunks[j], tops, carry)
        tops = tops_next
    out_t = jnp.concatenate([acc[:MLA_V] / acc[MLA_V:MLA_V + 1] for _, acc in carry], axis=0)
    o_ref[0] = out_t.T.astype(BF16)


def _attention(q, kvs):
    b, lq, _ = q.shape
    tq = min(ATT_TQ, lq)
    hs = HEADS_PER_STEP
    in_specs = [pl.BlockSpec((1, tq, hs * HEAD_PAD), lambda bi, hp, i: (bi, i, hp))]
    args = [q]
    for k, vt in kvs:
        n = k.shape[1]
        in_specs.append(pl.BlockSpec((1, n, hs * HEAD_PAD), lambda bi, hp, i: (bi, 0, hp)))
        in_specs.append(pl.BlockSpec((1, hs * V_ROWS, n), lambda bi, hp, i: (bi, hp, 0)))
        args += [k, vt]
    return pl.pallas_call(
        functools.partial(_attn_kernel, kv_lens=tuple(k.shape[1] for k, _ in kvs)),
        grid=(b, MLA_HEADS // hs, lq // tq),
        in_specs=in_specs,
        out_specs=pl.BlockSpec((1, tq, hs * MLA_V), lambda bi, hp, i: (bi, i, hp)),
        out_shape=jax.ShapeDtypeStruct((b, lq, MLA_HEADS * MLA_V), BF16),
        scratch_shapes=[pltpu.VMEM((2, hs, ATT_TK, tq), F32)],
        compiler_params=_params("parallel", "parallel", "parallel"),
        name="attention",
    )(*args)


def _s5_x_kernel(uc_ref, ul_ref, w_ref, xc_ref, xl_ref):
    xc_ref[...] = _dot(uc_ref[0], w_ref[0])
    xl_ref[...] = _dot(ul_ref[0], w_ref[0])


def _s5_chunk_inputs(u_ctx, u_lat, wx):
    g, _, n = u_lat.shape
    rows = lambda u: pl.BlockSpec((1, u.shape[1], n), lambda gi: (gi, 0, 0))
    cols = lambda u: pl.BlockSpec((u.shape[1], n), lambda gi: (0, gi))
    return pl.pallas_call(
        _s5_x_kernel,
        grid=(g,),
        in_specs=[rows(u_ctx), rows(u_lat), pl.BlockSpec((1, n, n), lambda gi: (gi, 0, 0))],
        out_specs=[cols(u_ctx), cols(u_lat)],
        out_shape=[jax.ShapeDtypeStruct((u.shape[1], g * n), F32) for u in (u_ctx, u_lat)],
        compiler_params=_params("parallel"),
        name="s5_chunk_inputs",
    )(u_ctx, u_lat, wx)


S5_SCAN_GROUPS = 16


def _s5_scan_kernel(xc_ref, xl_ref, lre_ref, lim_ref, sc_ref, sl_ref):
    ns = S5_STATE
    lre = lre_ref[...]
    lim = lim_ref[...]
    is_fwd = lax.broadcasted_iota(jnp.int32, lre.shape, 1) < ns

    def walk(x_ref, s_ref, carry):
        n = x_ref.shape[1]

        def step(i, carry):
            sre, sim = carry
            cf = i
            cb = n - 1 - i
            s_ref[0, cf, :, 0:ns] = sre[:, :ns]
            s_ref[0, cb, :, ns:2 * ns] = sre[:, ns:]
            s_ref[0, cf, :, 2 * ns:3 * ns] = sim[:, :ns]
            s_ref[0, cb, :, 3 * ns:] = sim[:, ns:]
            xf = x_ref[0, cf]
            xb = x_ref[0, cb]
            xre = jnp.where(is_fwd, xf[:, :2 * ns], xb[:, :2 * ns])
            xim = jnp.where(is_fwd, xf[:, 2 * ns:], xb[:, 2 * ns:])
            return sre * lre - sim * lim + xre, sre * lim + sim * lre + xim

        return lax.fori_loop(0, n, step, carry)

    zero = jnp.zeros(lre.shape, F32)
    walk(xl_ref, sl_ref, walk(xc_ref, sc_ref, (zero, zero)))


def _s5_scan(x_ctx, x_lat, p1, p2):
    b, _, g, w = x_lat.shape
    gb = S5_SCAN_GROUPS
    chunks = lambda x: pl.BlockSpec((1, x.shape[1], gb, w), lambda bi, gi: (bi, 0, gi, 0))
    pole = pl.BlockSpec((gb, w // 2), lambda bi, gi: (gi, 0))
    return pl.pallas_call(
        _s5_scan_kernel,
        grid=(b, g // gb),
        in_specs=[chunks(x_ctx), chunks(x_lat), pole, pole],
        out_specs=[chunks(x_ctx), chunks(x_lat)],
        out_shape=[jax.ShapeDtypeStruct(x.shape, F32) for x in (x_ctx, x_lat)],
        compiler_params=_params("parallel", "parallel"),
        name="s5_scan",
    )(x_ctx, x_lat, p1, p2)


def _s5_y_kernel(uc_ref, sc_ref, ul_ref, sl_ref, w_ref, yc_ref, yl_ref):
    n = ul_ref.shape[2]
    for u_ref, s_ref, y_ref in ((uc_ref, sc_ref, yc_ref), (ul_ref, sl_ref, yl_ref)):
        y = _dot(u_ref[0], w_ref[0, :n]) + _dot(s_ref[...].astype(BF16), w_ref[0, n:])
        y_ref[0] = y.astype(y_ref.dtype)


def _s5_outputs(u_ctx, s_ctx, u_lat, s_lat, wy):
    g, _, n = u_lat.shape
    rows = lambda u: pl.BlockSpec((1, u.shape[1], n), lambda gi: (gi, 0, 0))
    cols = lambda u: pl.BlockSpec((u.shape[1], n), lambda gi: (0, gi))
    return pl.pallas_call(
        _s5_y_kernel,
        grid=(g,),
        in_specs=[rows(u_ctx), cols(u_ctx), rows(u_lat), cols(u_lat),
                  pl.BlockSpec((1, 2 * n, n), lambda gi: (gi, 0, 0))],
        out_specs=[rows(u_ctx), rows(u_lat)],
        out_shape=[jax.ShapeDtypeStruct(u.shape, BF16) for u in (u_ctx, u_lat)],
        compiler_params=_params("parallel"),
        name="s5_outputs",
    )(u_ctx, s_ctx, u_lat, s_lat, wy)


def _cmul(ar, ai, br, bi):
    return ar * br - ai * bi, ar * bi + ai * br


def _s5_weights(lam_re, lam_im, log_dt, b_re, b_im, c_re, c_im):
    t = S5_CHUNK
    hi = lax.Precision.HIGHEST
    dt = jnp.exp(log_dt)[..., None]
    mag = jnp.exp(lam_re * dt)
    lb_re = mag * jnp.cos(lam_im * dt)
    lb_im = mag * jnp.sin(lam_im * dt)
    den = lam_re * lam_re + lam_im * lam_im
    nr = lb_re - 1.0
    fr = (nr * lam_re + lb_im * lam_im) / den
    fi = (lb_im * lam_re - nr * lam_im) / den
    bb_re = fr[..., None] * b_re - fi[..., None] * b_im
    bb_im = fr[..., None] * b_im + fi[..., None] * b_re
    pw = [(jnp.ones_like(lb_re), jnp.zeros_like(lb_im))]
    for _ in range(t):
        pw.append(_cmul(pw[-1][0], pw[-1][1], lb_re, lb_im))
    pw_re = jnp.stack([p[0] for p in pw])
    pw_im = jnp.stack([p[1] for p in pw])
    pb_re, pb_im = _cmul(pw_re[:t, ..., None], pw_im[:t, ..., None], bb_re[None], bb_im[None])
    kern = (jnp.einsum('dgpn,ldgnq->ldgpq', c_re, pb_re, precision=hi)
            - jnp.einsum('dgpn,ldgnq->ldgpq', c_im, pb_im, precision=hi))
    lag = jnp.arange(t)[None, :] - jnp.arange(t)[:, None]
    pick = lambda d: (d[..., None] == jnp.arange(t)).astype(F32)
    kf = jnp.einsum('stl,lgpq->stgpq', pick(lag), kern[:, 0], precision=hi)
    kb = jnp.einsum('stl,lgpq->stgpq', pick(-lag), kern[:, 1], precision=hi)
    g = lam_re.shape[1]
    n_lane = t * S5_GROUP
    toep = (kf + kb).transpose(2, 0, 4, 1, 3).reshape(g, n_lane, n_lane)

    def to_state(w):
        return w.transpose(1, 0, 3, 2).reshape(g, n_lane, S5_STATE)

    wx = jnp.concatenate([to_state(pb_re[::-1, 0]), to_state(pb_re[:, 1]),
                          to_state(pb_im[::-1, 0]), to_state(pb_im[:, 1])], axis=-1)

    def from_state(d, p_re, p_im):
        cr = c_re[d][None]
        ci = c_im[d][None]
        a = cr * p_re[:, :, None, :] - ci * p_im[:, :, None, :]
        bneg = -(cr * p_im[:, :, None, :] + ci * p_re[:, :, None, :])
        lanes = lambda w: w.transpose(1, 3, 0, 2).reshape(g, S5_STATE, n_lane)
        return lanes(a), lanes(bneg)

    f_re, f_im = from_state(0, pw_re[1:, 0], pw_im[1:, 0])
    b_re, b_im = from_state(1, pw_re[t:0:-1, 1], pw_im[t:0:-1, 1])
    wy = jnp.concatenate([toep, f_re, b_re, f_im, b_im], axis=1)
    lt_re, lt_im = pw_re[t], pw_im[t]
    p1 = jnp.concatenate([lt_re[0], lt_re[1]], axis=-1)
    p2 = jnp.concatenate([lt_im[0], lt_im[1]], axis=-1)
    return wx.astype(BF16), wy.astype(BF16), p1, p2


S5_RELAYOUT_CHUNKS = 64
LANE_GROUPS = LANE // S5_GROUP


def _to_groups_kernel(us_ref, u_ref, tok_ref):
    t, p = S5_CHUNK, S5_GROUP
    nc = u_ref.shape[1]
    tiles = S5_WIDTH // LANE
    for q in range(tiles):
        tok_ref[q] = us_ref[0, :, q * LANE:(q + 1) * LANE].astype(F32)
    for s in range(t):
        si = s % LANE_GROUPS
        for q in range(tiles):
            piece = tok_ref[q, pl.ds(s, nc, stride=t), :]
            for gi in range(LANE_GROUPS):
                shift = ((si - gi) * p) % LANE
                moved = pltpu.roll(piece, shift, 1) if shift else piece
                u_ref[q * LANE_GROUPS + gi, :, s * p:(s + 1) * p] = moved[:, si * p:(si + 1) * p].astype(BF16)


def _from_groups_kernel(y_ref, o_ref, step_ref, tok_ref):
    t, p = S5_CHUNK, S5_GROUP
    nc = y_ref.shape[1]
    for g in range(S5_GROUPS):
        q, gi = divmod(g, LANE_GROUPS)
        for so in range(t * p // LANE):
            piece = y_ref[g, :, so * LANE:(so + 1) * LANE].astype(F32)
            for si in range(LANE_GROUPS):
                shift = ((gi - si) * p) % LANE
                moved = pltpu.roll(piece, shift, 1) if shift else piece
                step_ref[so * LANE_GROUPS + si, q, :, gi * p:(gi + 1) * p] = moved[:, gi * p:(gi + 1) * p]
    for q in range(S5_WIDTH // LANE):
        for s in range(t):
            tok_ref[q, pl.ds(s, nc, stride=t), :] = step_ref[s, q]
        o_ref[0, :, q * LANE:(q + 1) * LANE] = tok_ref[q].astype(o_ref.dtype)


def _relayout_tile(n_chunks):
    return min(S5_RELAYOUT_CHUNKS, n_chunks)


def _s5_to_groups(us):
    b, n, w = us.shape
    t = S5_CHUNK
    nc = _relayout_tile(n // t)
    steps = n // t // nc
    return pl.pallas_call(
        _to_groups_kernel,
        grid=(b, steps),
        in_specs=[pl.BlockSpec((1, nc * t, w), lambda bi, i: (bi, i, 0))],
        out_specs=pl.BlockSpec((S5_GROUPS, nc, t * S5_GROUP), lambda bi, i: (0, bi * steps + i, 0)),
        out_shape=jax.ShapeDtypeStruct((S5_GROUPS, b * (n // t), t * S5_GROUP), BF16),
        scratch_shapes=[pltpu.VMEM((w // LANE, nc * t, LANE), F32)],
        compiler_params=_params("parallel", "parallel"),
        name="s5_to_groups",
    )(us)


def _s5_from_groups(y, b):
    g, r, tp = y.shape
    t = S5_CHUNK
    n = r // b * t
    nc = _relayout_tile(n // t)
    steps = n // t // nc
    w = g * S5_GROUP
    return pl.pallas_call(
        _from_groups_kernel,
        grid=(b, steps),
        in_specs=[pl.BlockSpec((g, nc, tp), lambda bi, i: (0, bi * steps + i, 0))],
        out_specs=pl.BlockSpec((1, nc * t, w), lambda bi, i: (bi, i, 0)),
        out_shape=jax.ShapeDtypeStruct((b, n, w), BF16),
        scratch_shapes=[pltpu.VMEM((t, w // LANE, nc, LANE), F32), pltpu.VMEM((w // LANE, nc * t, LANE), F32)],
        compiler_params=_params("parallel", "parallel"),
        name="s5_from_groups",
    )(y)


def _s5_layer(us_ctx, us_lat, w):
    b = us_ctx.shape[0]
    g = S5_GROUPS
    u_c, u_l = _s5_to_groups(us_ctx), _s5_to_groups(us_lat)
    x_c, x_l = _s5_chunk_inputs(u_c, u_l, w["wx"])
    as_chunks = lambda x: x.reshape(b, x.shape[0] // b, g, 2 * LANE)
    s_c, s_l = _s5_scan(as_chunks(x_c), as_chunks(x_l), w["p1"], w["p2"])
    y_c, y_l = _s5_outputs(u_c, s_c.reshape(x_c.shape), u_l, s_l.reshape(x_l.shape), w["wy"])
    return _s5_from_groups(y_c, b), _s5_from_groups(y_l, b)


FFN_CHUNK = FFN_HIDDEN // 2


def _mid_tail(x, y, m, wffn_ref, lng_ref, lnb_ref, x1_ref, a_ref, gt_ref):
    x1 = _layer_norm(DN_ALPHA * x + m[2:3] * y, lng_ref[...], lnb_ref[...])
    x1_ref[0] = x1
    uf = (x1 * (1.0 + m[4:5]) + m[3:4]).astype(BF16)
    for j in range(FFN_HIDDEN // FFN_CHUNK):
        sl = slice(j * FFN_CHUNK, (j + 1) * FFN_CHUNK)
        a_ref[0, :, sl] = _dot(uf, wffn_ref[:, sl]).astype(BF16)
        gsl = slice(FFN_HIDDEN + j * FFN_CHUNK, FFN_HIDDEN + (j + 1) * FFN_CHUNK)
        gt_ref[0, :, sl] = _dot(uf, wffn_ref[:, gsl]).astype(BF16)


def _gelu_tanh(x):
    return 0.5 * x * (1.0 + jnp.tanh(math.sqrt(2.0 / math.pi) * (x + 0.044715 * x * x * x)))


def _even_mid_kernel(x_ref, mod_ref, att_ref, ys_ref, us_ref, d_ref, wglu_ref, wout_ref, lng_ref, lnb_ref,
                     wffn_ref, x1_ref, a_ref, gt_ref):
    z = _gelu_tanh(ys_ref[0].astype(F32) + d_ref[...] * us_ref[0].astype(F32))
    s5 = z * _sigmoid(_dot(z.astype(BF16), wglu_ref[...]))
    n_att = att_ref.shape[2]
    y = _dot(att_ref[0], wout_ref[:n_att]) + _dot(s5.astype(BF16), wout_ref[n_att:])
    _mid_tail(x_ref[0], y, mod_ref[0], wffn_ref, lng_ref, lnb_ref, x1_ref, a_ref, gt_ref)


def _odd_mid_kernel(x_ref, mod_ref, of_ref, ob_ref, g_ref, hn_ref, wout_ref, lng_ref, lnb_ref,
                    wffn_ref, x1_ref, a_ref, gt_ref):
    o = of_ref[0] + ob_ref[0]
    g = g_ref[0]
    gate = g * _sigmoid(g)
    parts = []
    for hh in range(HG_HEADS):
        sl = slice(hh * HG_DV, (hh + 1) * HG_DV)
        parts.append((_rms(o[:, sl], hn_ref[...]) * gate[:, sl]).astype(BF16))
    y = _dot(jnp.concatenate(parts, axis=-1), wout_ref[...])
    _mid_tail(x_ref[0], y, mod_ref[0], wffn_ref, lng_ref, lnb_ref, x1_ref, a_ref, gt_ref)


MID_TM = 512


def _mid_call(kernel, name, x, mod, row_args, row_specs, const_args):
    b, l, d = x.shape
    tm = min(MID_TM, l)
    row = lambda n: pl.BlockSpec((1, tm, n), lambda bi, i: (bi, i, 0))
    in_specs = [row(d), pl.BlockSpec((1, 8, d), lambda bi, i: (bi, 0, 0))]
    in_specs += [spec(tm) for spec in row_specs]
    in_specs += [_const_spec(a.shape) for a in const_args]
    return pl.pallas_call(
        kernel,
        grid=(b, l // tm),
        in_specs=in_specs,
        out_specs=[row(d), row(FFN_HIDDEN), row(FFN_HIDDEN)],
        out_shape=[jax.ShapeDtypeStruct((b, l, d), F32),
                   jax.ShapeDtypeStruct((b, l, FFN_HIDDEN), BF16),
                   jax.ShapeDtypeStruct((b, l, FFN_HIDDEN), BF16)],
        compiler_params=_params("parallel", "parallel"),
        name=name,
    )(x, mod, *row_args, *const_args)


def _row_spec(n, lane_block=0):
    return lambda tm: pl.BlockSpec((1, tm, n), lambda bi, i: (bi, i, lane_block))


def _even_mid(x, mod, att, ys, us, w):
    return _mid_call(
        _even_mid_kernel, "even_mid", x, mod, [att, ys, us],
        [_row_spec(att.shape[2]), _row_spec(S5_WIDTH), _row_spec(S5_WIDTH)],
        [w["s5_d"], w["w_glu"], w["w_out"], w["ln_g0"], w["ln_b0"], w["ffn_w_in"]])


def _odd_mid(x, mod, o_f, o_b, hg, w):
    return _mid_call(
        _odd_mid_kernel, "odd_mid", x, mod, [o_f, o_b, hg],
        [_row_spec(HG_WIDTH), _row_spec(HG_WIDTH), _row_spec(HG_WIDTH, 4)],
        [w["hg_norm"], w["w_out"], w["ln_g0"], w["ln_b0"], w["ffn_w_in"]])


FFN_TM = 256
HALO = 8


def _ffn_out_kernel(a_ref, ap_ref, an_ref, gt_ref, x1_ref, mod_ref, cw_ref, cb_ref, wout_ref, lng_ref, lnb_ref,
                    o_ref):
    i = pl.program_id(1)
    tm = a_ref.shape[1]
    a = a_ref[0].astype(F32)
    row = lax.broadcasted_iota(jnp.int32, (HALO, 1), 0)
    before = jnp.where(i == 0, 0.0, ap_ref[0, HALO - 1:HALO, :].astype(F32))
    after = jnp.where(i == pl.num_programs(1) - 1, 0.0, an_ref[0, 0:1, :].astype(F32))
    a_prev = pltpu.roll(a, 1, 0)
    a_prev = jnp.concatenate([jnp.where(row == 0, before, a_prev[:HALO]), a_prev[HALO:]], axis=0)
    a_next = pltpu.roll(a, tm - 1, 0)
    a_next = jnp.concatenate([a_next[:tm - HALO], jnp.where(row == HALO - 1, after, a_next[tm - HALO:])], axis=0)
    cw = cw_ref[...]
    conv = cb_ref[...] + cw[0:1] * a_prev + cw[1:2] * a + cw[2:3] * a_next
    hidden = _silu(conv).astype(BF16) * gt_ref[0]
    f = _dot(hidden, wout_ref[...])
    m = mod_ref[0]
    o_ref[0] = _layer_norm(DN_ALPHA * x1_ref[0] + m[5:6] * f, lng_ref[...], lnb_ref[...])


def _ffn_out(a, gt, x1, mod, w):
    b, l, d = x1.shape
    tm = min(FFN_TM, l)
    per = tm // HALO
    last = l // HALO - 1
    row = lambda n: pl.BlockSpec((1, tm, n), lambda bi, i: (bi, i, 0))
    consts = [w["conv_w"], w["conv_b"], w["ffn_w_out"], w["ln_g1"], w["ln_b1"]]
    return pl.pallas_call(
        _ffn_out_kernel,
        grid=(b, l // tm),
        in_specs=[
            row(FFN_HIDDEN),
            pl.BlockSpec((1, HALO, FFN_HIDDEN), lambda bi, i: (bi, jnp.maximum(i * per - 1, 0), 0)),
            pl.BlockSpec((1, HALO, FFN_HIDDEN), lambda bi, i: (bi, jnp.minimum((i + 1) * per, last), 0)),
            row(FFN_HIDDEN), row(d),
            pl.BlockSpec((1, 8, d), lambda bi, i: (bi, 0, 0)),
        ] + [_const_spec(c.shape) for c in consts],
        out_specs=row(d),
        out_shape=jax.ShapeDtypeStruct((b, l, d), F32),
        compiler_params=_params("parallel", "parallel"),
        name="ffn_out",
    )(a, a, a, gt, x1, mod, *consts)


ODD_TM = 256


def _odd_in_kernel(x_ref, mod_ref, w_ref, o_ref):
    m = mod_ref[0]
    u = (x_ref[0] * (1.0 + m[1:2]) + m[0:1]).astype(BF16)
    for j in range(HG_IN // HG_WIDTH):
        sl = slice(j * HG_WIDTH, (j + 1) * HG_WIDTH)
        o_ref[0, :, sl] = _dot(u, w_ref[:, sl])


def _odd_in(x, mod, w_in):
    b, l, d = x.shape
    tm = min(ODD_TM, l)
    return pl.pallas_call(
        _odd_in_kernel,
        grid=(b, l // tm),
        in_specs=[pl.BlockSpec((1, tm, d), lambda bi, i: (bi, i, 0)),
                  pl.BlockSpec((1, 8, d), lambda bi, i: (bi, 0, 0)),
                  _const_spec(w_in.shape)],
        out_specs=pl.BlockSpec((1, tm, HG_IN), lambda bi, i: (bi, i, 0)),
        out_shape=jax.ShapeDtypeStruct((b, l, HG_IN), F32),
        compiler_params=_params("parallel", "parallel"),
        name="odd_in",
    )(x, mod, w_in)


def _hgrn_direction(q, pre, v, lb, state, level, reverse):
    c = q.shape[0]
    row = lax.broadcasted_iota(jnp.int32, q.shape, 0)
    pos = (c - 1 - row) if reverse else row

    def prev(x, k):
        return pltpu.roll(x, (c - k) if reverse else k, 0)

    def nxt(x, k):
        return pltpu.roll(x, k if reverse else (c - k), 0)

    f = lb + (1.0 - lb) * _sigmoid(pre)
    key = 1.0 - f
    cum = jnp.log(f)
    k = 1
    while k < c:
        cum = cum + jnp.where(pos >= k, prev(cum, k), 0.0)
        k *= 2
    last = cum[0:1] if reverse else cum[c - 1:c]

    qb = q.astype(BF16)
    kb = key.astype(BF16)
    scores = jnp.where(level == 0, _dot_nt(qb, kb), 0.0)
    z = cum
    half = 1
    lvl = 1
    while half < c:
        ref = jnp.where((pos & half) == 0, z, prev(z, half))
        e = jnp.exp(-jnp.abs(cum - ref))
        x = (jnp.where((pos & half) != 0, q, key) * e).astype(BF16)
        scores = jnp.where(level == lvl, _dot_nt(x, x), scores)
        z = jnp.where((pos & half) != 0, z, nxt(z, half))
        half *= 2
        lvl += 1

    o = _dot(scores.astype(BF16), v)
    o = o + _dot_nt((q * jnp.exp(cum)).astype(BF16), state.astype(BF16))
    kd = (key * jnp.exp(last - cum)).astype(BF16)
    new_state = jnp.exp(last) * state + _dot_tn(v, kd)
    return o, new_state


def _hgrn_kernel(qf_ref, pf_ref, vf_ref, qb_ref, pb_ref, vb_ref, lb_ref, lvf_ref, lvb_ref, s0_ref,
                 of_ref, ob_ref, sfin_ref, st_ref):
    ci = pl.program_id(2)

    @pl.when(ci == 0)
    def _():
        st_ref[...] = s0_ref[0, 0]

    o, s = _hgrn_direction(qf_ref[0], pf_ref[0], vf_ref[0].astype(BF16), lb_ref[0], st_ref[0],
                           lvf_ref[...], False)
    of_ref[0] = o
    st_ref[0] = s
    o, s = _hgrn_direction(qb_ref[0], pb_ref[0], vb_ref[0].astype(BF16), lb_ref[1], st_ref[1],
                           lvb_ref[...], True)
    ob_ref[0] = o
    st_ref[1] = s

    @pl.when(ci == pl.num_programs(2) - 1)
    def _():
        sfin_ref[0, 0] = st_ref[...]


def _hgrn_levels(c):
    p = jnp.arange(c, dtype=jnp.int32)
    x = p[:, None] ^ p[None, :]
    lvl = jnp.where(x == 0, 0, 32 - lax.clz(x))
    fwd = jnp.where(p[:, None] >= p[None, :], lvl, -1)
    bwd = jnp.where(p[:, None] <= p[None, :], lvl, -1)
    return fwd.astype(jnp.int32), bwd.astype(jnp.int32)


def _hgrn(hg, lb, s0):
    b, l, _ = hg.shape
    c = HG_CHUNK
    n = l // c
    nh = HG_HEADS
    lvf, lvb = _hgrn_levels(c)
    fw = lambda blk: pl.BlockSpec((1, c, HG_DK), lambda bi, h, ci: (bi, ci, blk * nh + h))
    bw = lambda blk: pl.BlockSpec((1, c, HG_DK), lambda bi, h, ci: (bi, n - 1 - ci, blk * nh + h))
    state_spec = pl.BlockSpec((1, 1, 2, HG_DV, HG_DK), lambda bi, h, ci: (bi, h, 0, 0, 0))
    return pl.pallas_call(
        _hgrn_kernel,
        grid=(b, nh, n),
        in_specs=[fw(0), fw(1), fw(3), bw(0), bw(2), bw(3),
                  pl.BlockSpec((2, 1, HG_DK), lambda bi, h, ci: (0, 0, h)),
                  pl.BlockSpec((c, c), lambda bi, h, ci: (0, 0)),
                  pl.BlockSpec((c, c), lambda bi, h, ci: (0, 0)),
                  state_spec],
        out_specs=[pl.BlockSpec((1, c, HG_DV), lambda bi, h, ci: (bi, ci, h)),
                   pl.BlockSpec((1, c, HG_DV), lambda bi, h, ci: (bi, n - 1 - ci, h)),
                   state_spec],
        out_shape=[jax.ShapeDtypeStruct((b, l, HG_WIDTH), F32),
                   jax.ShapeDtypeStruct((b, l, HG_WIDTH), F32),
                   jax.ShapeDtypeStruct((b, nh, 2, HG_DV, HG_DK), F32)],
        scratch_shapes=[pltpu.VMEM((2, HG_DV, HG_DK), F32)],
        compiler_params=_params("parallel", "parallel", "arbitrary"),
        name="hgrn",
    )(hg, hg, hg, hg, hg, hg, lb, lvf, lvb, s0)


HG_DIRECT_CHUNK = 64
HG_DIRECT_BLOCK = 1024
HG_DIRECT_MAX_EXPONENT = 60.0
HG_DIRECT_MIN_LB = math.exp(-HG_DIRECT_MAX_EXPONENT / (HG_DIRECT_CHUNK // 2))


def _split3(x):
    hi = x.astype(BF16)
    r = x - hi.astype(F32)
    mid = r.astype(BF16)
    lo = (r - mid.astype(F32)).astype(BF16)
    return hi, mid, lo


def _hgrn_direct_block(q_ref, p_ref, v_ref, o_ref, lb, state, tri3, visible, reverse):
    c = HG_DIRECT_CHUNK
    n_sub = q_ref.shape[1] // c
    half = c // 2
    order = list(range(n_sub - 1, -1, -1) if reverse else range(n_sub))
    rows = [slice(j * c, (j + 1) * c) for j in order]
    fs = [lb + (1.0 - lb) * _sigmoid(p_ref[0, r, :]) for r in rows]
    cums = [_dot(tri3, jnp.concatenate(_split3(jnp.log(f)), axis=0)) for f in fs]
    mids = [cum[half:half + 1] if reverse else cum[half - 1:half] for cum in cums]
    lasts = [cum[0:1] if reverse else cum[c - 1:c] for cum in cums]
    qts = [(q_ref[0, r, :] * jnp.exp(cum - mid)).astype(BF16) for r, cum, mid in zip(rows, cums, mids)]
    kts = [((1.0 - f) * jnp.exp(mid - cum)).astype(BF16) for f, cum, mid in zip(fs, cums, mids)]
    vs = [v_ref[0, r, :].astype(BF16) for r in rows]
    scores = [jnp.where(visible, _dot_nt(qt, kt), 0.0).astype(BF16) for qt, kt in zip(qts, kts)]
    adds = [jnp.exp(last - mid) * _dot_tn(v, kt) for last, mid, v, kt in zip(lasts, mids, vs, kts)]
    intra = [_dot(sc, v) for sc, v in zip(scores, vs)]
    entering = []
    for mid, last, add in zip(mids, lasts, adds):
        entering.append((state * jnp.exp(mid)).astype(BF16))
        state = jnp.exp(last) * state + add
    for r, qt, o, s_in in zip(rows, qts, intra, entering):
        o_ref[0, r, :] = o + _dot_nt(qt, s_in)
    return state


def _hgrn_direct_kernel(qf_ref, pf_ref, vf_ref, qb_ref, pb_ref, vb_ref, lb_ref, trif_ref, trib_ref, s0_ref,
                        of_ref, ob_ref, sfin_ref, st_ref):
    ci = pl.program_id(2)
    c = HG_DIRECT_CHUNK

    @pl.when(ci == 0)
    def _():
        st_ref[...] = s0_ref[0, 0]

    row = lax.broadcasted_iota(jnp.int32, (c, c), 0)
    col = lax.broadcasted_iota(jnp.int32, (c, c), 1)
    st_ref[0] = _hgrn_direct_block(qf_ref, pf_ref, vf_ref, of_ref, lb_ref[0], st_ref[0], trif_ref[...],
                                   row >= col, False)
    st_ref[1] = _hgrn_direct_block(qb_ref, pb_ref, vb_ref, ob_ref, lb_ref[1], st_ref[1], trib_ref[...],
                                   row <= col, True)

    @pl.when(ci == pl.num_programs(2) - 1)
    def _():
        sfin_ref[0, 0] = st_ref[...]


def _hgrn_direct(hg, lb, s0):
    b, l, _ = hg.shape
    c = HG_DIRECT_CHUNK
    blk_rows = min(HG_DIRECT_BLOCK, l)
    n = l // blk_rows
    nh = HG_HEADS
    p = jnp.arange(c)
    lower_tri = (p[:, None] >= p[None, :]).astype(BF16)
    trif = jnp.concatenate([lower_tri] * 3, axis=1)
    trib = jnp.concatenate([lower_tri.T] * 3, axis=1)
    fw = lambda blk: pl.BlockSpec((1, blk_rows, HG_DK), lambda bi, h, ci: (bi, ci, blk * nh + h))
    bw = lambda blk: pl.BlockSpec((1, blk_rows, HG_DK), lambda bi, h, ci: (bi, n - 1 - ci, blk * nh + h))
    state_spec = pl.BlockSpec((1, 1, 2, HG_DV, HG_DK), lambda bi, h, ci: (bi, h, 0, 0, 0))
    return pl.pallas_call(
        _hgrn_direct_kernel,
        grid=(b, nh, n),
        in_specs=[fw(0), fw(1), fw(3), bw(0), bw(2), bw(3),
                  pl.BlockSpec((2, 1, HG_DK), lambda bi, h, ci: (0, 0, h)),
                  pl.BlockSpec((c, 3 * c), lambda bi, h, ci: (0, 0)),
                  pl.BlockSpec((c, 3 * c), lambda bi, h, ci: (0, 0)),
                  state_spec],
        out_specs=[pl.BlockSpec((1, blk_rows, HG_DV), lambda bi, h, ci: (bi, ci, h)),
                   pl.BlockSpec((1, blk_rows, HG_DV), lambda bi, h, ci: (bi, n - 1 - ci, h)),
                   state_spec],
        out_shape=[jax.ShapeDtypeStruct((b, l, HG_WIDTH), F32),
                   jax.ShapeDtypeStruct((b, l, HG_WIDTH), F32),
                   jax.ShapeDtypeStruct((b, nh, 2, HG_DV, HG_DK), F32)],
        scratch_shapes=[pltpu.VMEM((2, HG_DV, HG_DK), F32)],
        compiler_params=_params("parallel", "parallel", "arbitrary"),
        name="hgrn_direct",
    )(hg, hg, hg, hg, hg, hg, lb, trif, trib, s0)


def _hgrn_auto(hg, lb, s0):
    return lax.cond(jnp.min(lb) >= HG_DIRECT_MIN_LB,
                    lambda: tuple(_hgrn_direct(hg, lb, s0)), lambda: tuple(_hgrn(hg, lb, s0)))


def _rope_tables(length):
    rows = length // GRID_W
    row = np.repeat(np.arange(rows, dtype=np.float64), GRID_W)
    col = np.tile(np.arange(GRID_W, dtype=np.float64), rows)
    n_freq = MLA_ROPE // 4
    inv = ROPE_BASE ** (-np.arange(n_freq, dtype=np.float64) / n_freq)
    ar = row[:, None] * inv
    ac = col[:, None] * inv
    ang = np.concatenate([ar, ar, ac, ac], axis=-1)
    return np.cos(ang), np.sin(ang)


def _head_tables(cos, sin):
    n = cos.shape[0]
    pad = np.zeros((n, HEAD_PAD - MLA_NOPE - MLA_ROPE))
    return (jnp.asarray(np.concatenate([np.ones((n, MLA_NOPE)), cos, pad], -1), F32),
            jnp.asarray(np.concatenate([np.zeros((n, MLA_NOPE)), sin, pad], -1), F32))


def _rotate_cols(w):
    ws = w.reshape(w.shape[:-1] + (2, 2, MLA_ROPE // 4))
    return jnp.stack([-ws[..., 1, :], ws[..., 0, :]], axis=-2).reshape(w.shape)


def _even_weights(ev_w_in, q_norm, w_uq, kv_norm, w_ukv):
    c0 = MLA_Q_LORA
    c1 = c0 + MLA_KV_LORA
    c2 = c1 + MLA_ROPE
    d = ev_w_in.shape[0]
    w_kr = ev_w_in[:, c1:c2]
    lo = jnp.zeros((d, MLA_NOPE), F32)
    hi = jnp.zeros((d, HEAD_PAD - MLA_NOPE - MLA_ROPE), F32)
    w_in = jnp.concatenate([ev_w_in[:, :c1], ev_w_in[:, c2:], lo, w_kr, hi, lo, _rotate_cols(w_kr), hi], axis=-1)
    wq = w_uq.reshape(c0, MLA_HEADS, MLA_NOPE + MLA_ROPE)
    zq = jnp.zeros((c0, MLA_HEADS, HEAD_PAD - MLA_NOPE - MLA_ROPE), F32)
    wq_pad = jnp.concatenate([wq, zq], -1).reshape(c0, MLA_HEADS * HEAD_PAD)
    wq_rot = jnp.concatenate([jnp.zeros((c0, MLA_HEADS, MLA_NOPE), F32), _rotate_cols(wq[..., MLA_NOPE:]), zq],
                             -1).reshape(c0, MLA_HEADS * HEAD_PAD)
    wkv = w_ukv.reshape(MLA_KV_LORA, MLA_HEADS, MLA_NOPE + MLA_V)
    wk = jnp.concatenate([wkv[..., :MLA_NOPE], jnp.zeros((MLA_KV_LORA, MLA_HEADS, HEAD_PAD - MLA_NOPE), F32)],
                         -1).reshape(MLA_KV_LORA, MLA_HEADS * HEAD_PAD)
    wv_t = wkv[..., MLA_NOPE:].reshape(MLA_KV_LORA, MLA_HEADS * MLA_V).T
    return {"w_in": w_in.astype(BF16), "q_norm": q_norm.reshape(1, -1), "kv_norm": kv_norm.reshape(1, -1),
            "wq": wq_pad.astype(BF16), "wq_rot": wq_rot.astype(BF16), "wk": wk.astype(BF16),
            "wv_t": wv_t.astype(BF16)}


def _layer_mod(mod, layer, b):
    d = D_MODEL
    rows = mod[layer, :, :6 * d].reshape(8, 6, d)
    pad = jnp.zeros((2, d), F32)
    lat = jnp.stack([jnp.concatenate([rows[bi], pad], 0) for bi in range(b)])
    ctx = jnp.broadcast_to(jnp.concatenate([rows[b], pad], 0)[None], (b, 8, d))
    return lat, ctx


def kernel(x, c, ctx, c_ctx, ada_w, ada_b, ln_g, ln_b, ffn_w_in, ffn_conv_w, ffn_conv_b, ffn_w_out, ev_w_in,
           mla_q_norm, mla_w_uq, mla_kv_norm, mla_w_ukv, s5_lam_re, s5_lam_im, s5_log_dt, s5_b_re, s5_b_im,
           s5_c_re, s5_c_im, s5_d, s5_w_glu, ev_w_out, hg_w_in, hg_lb, hg_norm, hg_w_out):
    b, l, d = x.shape
    lc = ctx.shape[1]
    assert b + 1 == N_COND and d == D_MODEL and l % 512 == 0 and lc % 256 == 0

    cond_t = jnp.concatenate([c, c_ctx[None], jnp.zeros((8 - N_COND, d), F32)], 0).T
    mod = _adaln(cond_t, ada_w, ada_b)

    def ffn_weights(layer):
        return {"ln_g0": ln_g[layer, 0].reshape(1, d), "ln_b0": ln_b[layer, 0].reshape(1, d),
                "ln_g1": ln_g[layer, 1].reshape(1, d), "ln_b1": ln_b[layer, 1].reshape(1, d),
                "ffn_w_in": ffn_w_in[layer].astype(BF16),
                "conv_w": jnp.concatenate([ffn_conv_w[layer], jnp.zeros((5, FFN_HIDDEN), F32)], 0),
                "conv_b": ffn_conv_b[layer].reshape(1, FFN_HIDDEN),
                "ffn_w_out": ffn_w_out[layer].astype(BF16)}

    mod_lat, mod_ctx = _layer_mod(mod, 0, b)
    we = _even_weights(ev_w_in[0], mla_q_norm[0], mla_w_uq[0], mla_kv_norm[0], mla_w_ukv[0])
    cos_l, sin_l = _head_tables(*_rope_tables(l))
    cos_c, sin_c = _head_tables(np.ones((lc, MLA_ROPE)), np.zeros((lc, MLA_ROPE)))
    q_l, k_l, v_l, us_l = _even_in(x, mod_lat, we, cos_l, sin_l, min(512, l))
    q_c, k_c, v_c, us_c = _even_in(ctx, mod_ctx, we, cos_c, sin_c, min(512, lc))
    att_l = _attention(q_l, [(k_l, v_l), (k_c, v_c)])
    att_c = _attention(q_c, [(k_c, v_c)])
    wx, wy, p1, p2 = _s5_weights(s5_lam_re[0], s5_lam_im[0], s5_log_dt[0], s5_b_re[0], s5_b_im[0],
                                 s5_c_re[0], s5_c_im[0])
    ys_c, ys_l = _s5_layer(us_c, us_l, {"wx": wx, "wy": wy, "p1": p1, "p2": p2})
    w0 = ffn_weights(0)
    w0.update({"s5_d": s5_d[0].reshape(1, S5_WIDTH), "w_glu": s5_w_glu[0].astype(BF16),
               "w_out": ev_w_out[0].astype(BF16)})
    x1, a, gt = _even_mid(x, mod_lat, att_l, ys_l, us_l, w0)
    x = _ffn_out(a, gt, x1, mod_lat, w0)
    c1, a, gt = _even_mid(ctx, mod_ctx, att_c, ys_c, us_c, w0)
    ctx = _ffn_out(a, gt, c1, mod_ctx, w0)

    mod_lat, mod_ctx = _layer_mod(mod, 1, b)
    sm = jax.nn.softmax(hg_lb, axis=0)
    lower = (jnp.cumsum(sm, axis=0) - sm[0])[1].reshape(2, 1, HG_WIDTH)
    w_hg = hg_w_in[0].astype(BF16)
    hg_c = _odd_in(ctx, mod_ctx, w_hg)
    hg_l = _odd_in(x, mod_lat, w_hg)
    zero_state = jnp.zeros((b, HG_HEADS, 2, HG_DV, HG_DK), F32)
    _, _, s_ctx = _hgrn_auto(hg_c, lower, zero_state)
    o_f, o_b, _ = _hgrn_auto(hg_l, lower, s_ctx)
    w1 = ffn_weights(1)
    w1.update({"hg_norm": hg_norm[0].reshape(1, HG_DV), "w_out": hg_w_out[0].astype(BF16)})
    x1, a, gt = _odd_mid(x, mod_lat, o_f, o_b, hg_l, w1)
    return _ffn_out(a, gt, x1, mod_lat, w1)
```
